```python
import jax, jax.numpy as jnp
from jax import lax
import numpy as np

D_MODEL = 1024
BATCH = 4
SEQ = 4096
DEPTH = 2

N_MIXERS = 2
CONV_WIDTH = 3
MLA_HEADS = 8
Q_LORA = 384
KV_LORA = 256
QK_NOPE = 128
QK_ROPE = 64
QK_HEAD = QK_NOPE + QK_ROPE
V_HEAD = 128
ROPE_THETA = 10000.0
ATTN_BLOCK = 128
PEER_HEADS = 8
PEER_KEYS = 128
PEER_EXPERTS = PEER_KEYS * PEER_KEYS
PEER_DKEY = 256
PEER_HALF = PEER_DKEY // 2
PEER_TOPK = 16
PEER_CHUNK = 128
NORM_EPS = 1e-6

kernel_name = "hybrid_shortconv_mla_peer_adaln"


def rms_norm(x, g):
    xf = x.astype(jnp.float32)
    y = xf * lax.rsqrt(jnp.mean(xf * xf, axis=-1, keepdims=True) + NORM_EPS)
    return (y * g.astype(jnp.float32)).astype(x.dtype)


def modulate(x, g, shift, scale):
    return rms_norm(x, g) * (1 + scale[:, None, :]) + shift[:, None, :]


def short_conv_mixer(h, w_in, w_conv, w_out):
    b_gate, c_gate, v = jnp.split(h @ w_in, 3, axis=-1)
    u = c_gate * v
    conv = lax.conv_general_dilated(
        u, w_conv[:, None, :].astype(u.dtype), window_strides=(1,),
        padding=[(CONV_WIDTH - 1, 0)], dimension_numbers=("NWC", "WIO", "NWC"),
        feature_group_count=D_MODEL)
    return (b_gate * conv) @ w_out


def apply_rope(x, cos, sin):
    half = x.shape[-1] // 2
    x1, x2 = x[..., :half], x[..., half:]
    return jnp.concatenate([x1 * cos - x2 * sin, x1 * sin + x2 * cos], axis=-1)


def mla_mixer(h, positions, w_in, q_a_g, kv_a_g, w_qb, w_kvb, q_g, k_g, w_out):
    B, S, _ = h.shape
    q_lat, kv_lat, k_rope = jnp.split(h @ w_in, [Q_LORA, Q_LORA + KV_LORA], axis=-1)
    q = (rms_norm(q_lat, q_a_g) @ w_qb).reshape(B, S, MLA_HEADS, QK_HEAD)
    kv = (rms_norm(kv_lat, kv_a_g) @ w_kvb).reshape(B, S, MLA_HEADS, QK_NOPE + V_HEAD)
    k_nope, v = kv[..., :QK_NOPE], kv[..., QK_NOPE:]
    k = jnp.concatenate(
        [k_nope, jnp.broadcast_to(k_rope[:, :, None, :], (B, S, MLA_HEADS, QK_ROPE))], axis=-1)
    q = rms_norm(q, q_g)
    k = rms_norm(k, k_g)
    inv_freq = ROPE_THETA ** (-jnp.arange(0, QK_ROPE, 2, dtype=jnp.float32) / QK_ROPE)
    ang = positions[..., None].astype(jnp.float32) * inv_freq
    cos = jnp.cos(ang)[:, :, None, :].astype(h.dtype)
    sin = jnp.sin(ang)[:, :, None, :].astype(h.dtype)
    q = jnp.concatenate([q[..., :QK_NOPE], apply_rope(q[..., QK_NOPE:], cos, sin)], axis=-1)
    k = jnp.concatenate([k[..., :QK_NOPE], apply_rope(k[..., QK_NOPE:], cos, sin)], axis=-1)
    scale = QK_HEAD ** -0.5
    outs = []
    for start in range(0, S, ATTN_BLOCK):
        end = start + ATTN_BLOCK
        s = jnp.einsum("bqhd,bkhd->bhqk", q[:, start:end], k[:, :end]).astype(jnp.float32) * scale
        mask = (start + jnp.arange(ATTN_BLOCK))[:, None] >= jnp.arange(end)[None, :]
        p = jax.nn.softmax(jnp.where(mask, s, -jnp.inf), axis=-1).astype(v.dtype)
        outs.append(jnp.einsum("bhqk,bkhd->bqhd", p, v[:, :end]))
    o = jnp.concatenate(outs, axis=1).reshape(B, S, MLA_HEADS * V_HEAD)
    return o @ w_out


def peer_mixer(h, w_q, subkeys, u_tab, v_tab):
    B, S, D = h.shape
    tokens = h.reshape(-1, PEER_CHUNK, D)

    def chunk_fn(xc):
        C = xc.shape[0]
        q = (xc @ w_q).reshape(C, PEER_HEADS, 2, PEER_HALF)
        s = jnp.einsum("chpd,hpnd->chpn", q, subkeys).astype(jnp.float32)
        top_s, top_i = lax.top_k(s, PEER_TOPK)
        cand = top_s[:, :, 0, :, None] + top_s[:, :, 1, None, :]
        best_s, best_pos = lax.top_k(cand.reshape(C, PEER_HEADS, PEER_TOPK * PEER_TOPK), PEER_TOPK)
        ia = jnp.take_along_axis(top_i[:, :, 0], best_pos // PEER_TOPK, axis=-1)
        ib = jnp.take_along_axis(top_i[:, :, 1], best_pos % PEER_TOPK, axis=-1)
        expert = ia * PEER_KEYS + ib
        g = jax.nn.softmax(best_s, axis=-1).astype(xc.dtype)
        act = jax.nn.gelu(jnp.einsum("chkd,cd->chk", u_tab[expert], xc), approximate=False) * g
        return jnp.einsum("chk,chkd->cd", act, v_tab[expert])

    return lax.map(chunk_fn, tokens).reshape(B, S, D)


def setup_inputs(seed: int = 0) -> dict:
    key = jax.random.key(seed)
    ks = jax.random.split(key, 24)
    f32 = jnp.float32
    n_conv = (DEPTH + N_MIXERS - 1) // N_MIXERS
    n_mla = DEPTH // N_MIXERS
    D = D_MODEL

    def nrm(k, shape, std):
        return jax.random.normal(k, shape, f32) * std

    def gain(k, shape):
        return 1.0 + 0.02 * jax.random.normal(k, shape, f32)

    x = jax.random.normal(ks[0], (BATCH, SEQ, D), f32)
    c = jax.random.normal(ks[1], (BATCH, D), f32)
    positions = jnp.broadcast_to(jnp.arange(SEQ, dtype=jnp.int32)[None, :], (BATCH, SEQ))
    return {
        "x": x,
        "c": c,
        "positions": positions,
        "tok_norm": gain(ks[2], (DEPTH, D)),
        "ch_norm": gain(ks[3], (DEPTH, D)),
        "ada_w": nrm(ks[4], (DEPTH, D, 6 * D), 0.5 * D ** -0.5),
        "ada_b": nrm(ks[5], (DEPTH, 6 * D), 0.02),
        "conv_in_w": nrm(ks[6], (n_conv, D, 3 * D), D ** -0.5),
        "conv_w": nrm(ks[7], (n_conv, CONV_WIDTH, D), CONV_WIDTH ** -0.5),
        "conv_out_w": nrm(ks[8], (n_conv, D, D), D ** -0.5),
        "mla_in_w": nrm(ks[9], (n_mla, D, Q_LORA + KV_LORA + QK_ROPE), D ** -0.5),
        "q_a_norm": gain(ks[10], (n_mla, Q_LORA)),
        "kv_a_norm": gain(ks[11], (n_mla, KV_LORA)),
        "q_b_w": nrm(ks[12], (n_mla, Q_LORA, MLA_HEADS * QK_HEAD), Q_LORA ** -0.5),
        "kv_b_w": nrm(ks[13], (n_mla, KV_LORA, MLA_HEADS * (QK_NOPE + V_HEAD)), KV_LORA ** -0.5),
        "q_norm": gain(ks[14], (n_mla, QK_HEAD)),
        "k_norm": gain(ks[15], (n_mla, QK_HEAD)),
        "mla_out_w": nrm(ks[16], (n_mla, MLA_HEADS * V_HEAD, D), (MLA_HEADS * V_HEAD) ** -0.5),
        "peer_q_w": nrm(ks[17], (DEPTH, D, PEER_HEADS * PEER_DKEY), D ** -0.5),
        "peer_subkeys": nrm(ks[18], (DEPTH, PEER_HEADS, 2, PEER_KEYS, PEER_HALF), PEER_HALF ** -0.5),
        "peer_u": nrm(ks[19], (DEPTH, PEER_EXPERTS, D), D ** -0.5),
        "peer_v": nrm(ks[20], (DEPTH, PEER_EXPERTS, D), 0.5),
    }


def reference(x, c, positions, tok_norm, ch_norm, ada_w, ada_b, conv_in_w, conv_w, conv_out_w,
              mla_in_w, q_a_norm, kv_a_norm, q_b_w, kv_b_w, q_norm, k_norm, mla_out_w,
              peer_q_w, peer_subkeys, peer_u, peer_v):
    c_act = jax.nn.silu(c)
    for i in range(DEPTH):
        mod = c_act @ ada_w[i] + ada_b[i]
        sh1, sc1, g1, sh2, sc2, g2 = jnp.split(mod, 6, axis=-1)
        h = modulate(x, tok_norm[i], sh1, sc1)
        j = i // N_MIXERS
        if i % N_MIXERS == 0:
            y = short_conv_mixer(h, conv_in_w[j], conv_w[j], conv_out_w[j])
        else:
            y = mla_mixer(h, positions, mla_in_w[j], q_a_norm[j], kv_a_norm[j], q_b_w[j],
                          kv_b_w[j], q_norm[j], k_norm[j], mla_out_w[j])
        x = x + g1[:, None, :] * y
        h = modulate(x, ch_norm[i], sh2, sc2)
        x = x + g2[:, None, :] * peer_mixer(h, peer_q_w[i], peer_subkeys[i], peer_u[i], peer_v[i])
    return x
```

```python
import functools
import math

import jax
import jax.numpy as jnp
from jax import lax
from jax.experimental import pallas as pl
from jax.experimental.pallas import tpu as pltpu

F32 = jnp.float32
BF16 = jnp.bfloat16

D_MODEL = 1024
MLA_HEADS = 8
Q_LORA = 384
KV_LORA = 256
QK_NOPE = 128
QK_ROPE = 64
QK_HEAD = QK_NOPE + QK_ROPE
QK_PAD = 256
V_HEAD = 128
ROPE_THETA = 10000.0
PEER_HEADS = 8
PEER_KEYS = 128
PEER_EXPERTS = PEER_KEYS * PEER_KEYS
PEER_HALF = 128
PEER_TOPK = 16
NORM_EPS = 1e-6

LANES = 128
SUBLANES = 8
VMEM_LIMIT = 56 * 1024 * 1024

NEG_INF = float("-inf")
_NT = (((1,), (1,)), ((), ()))


def _cparams(n_axes):
    return pltpu.CompilerParams(
        dimension_semantics=("arbitrary",) * n_axes, vmem_limit_bytes=VMEM_LIMIT)


def _split_bf16(a):
    hi = a.astype(BF16)
    lo = (a - hi.astype(F32)).astype(BF16)
    return hi, lo


def _dot3(a_hi, a_lo, b_hi, b_lo, dims=None):
    if dims is None:
        f = lambda a, b: jnp.dot(a, b, preferred_element_type=F32)
    else:
        f = lambda a, b: lax.dot_general(a, b, dims, preferred_element_type=F32)
    return f(a_hi, b_hi) + (f(a_hi, b_lo) + f(a_lo, b_hi))


def _modulate(x, g, shift, scale):
    ms = jnp.mean(x * x, axis=-1, keepdims=True)
    y = x * lax.rsqrt(ms + NORM_EPS)
    return (y * g) * (1.0 + scale) + shift


def _ada_kernel(c_ref, w_ref, b_ref, o_ref):
    c = c_ref[...]
    ca = c * jax.nn.sigmoid(c)
    o_ref[0] = jnp.dot(ca, w_ref[0], preferred_element_type=F32,
                       precision=lax.Precision.HIGHEST) + b_ref[0]


def _ada(c_pad, ada_w, ada_b):
    depth, d, d6 = ada_w.shape
    bp = c_pad.shape[0]
    nb = d6 // d
    return pl.pallas_call(
        _ada_kernel,
        grid=(depth, nb),
        in_specs=[
            pl.BlockSpec((bp, d), lambda i, j: (0, 0)),
            pl.BlockSpec((1, d, d), lambda i, j: (i, 0, j)),
            pl.BlockSpec((1, 1, d), lambda i, j: (i, 0, j)),
        ],
        out_specs=pl.BlockSpec((1, bp, d), lambda i, j: (i, 0, j)),
        out_shape=jax.ShapeDtypeStruct((depth, bp, d6), F32),
        compiler_params=_cparams(2),
        name="ada",
    )(c_pad, ada_w, ada_b.reshape(depth, 1, d6))


def _conv_layer_kernel(x_ref, mod_ref, tn_ref, cn_ref, win_ref, cw_ref, wout_ref,
                       x1_ref, h2_ref, ubuf, *, ts):
    d = D_MODEL
    x = x_ref[0]
    m = mod_ref[0]
    sh1, sc1, g1, sh2, sc2 = m[0:1], m[1:2], m[2:3], m[3:4], m[4:5]
    h = _modulate(x, tn_ref[...], sh1, sc1)
    bcv = jnp.dot(h.astype(BF16), win_ref[...], preferred_element_type=F32)
    b_gate = bcv[:, :d]
    u = bcv[:, d:2 * d] * bcv[:, 2 * d:]

    @pl.when(pl.program_id(1) == 0)
    def _():
        ubuf[0:SUBLANES, :] = jnp.zeros((SUBLANES, d), F32)

    ubuf[SUBLANES:SUBLANES + ts, :] = u
    u1 = ubuf[SUBLANES - 1:SUBLANES - 1 + ts, :]
    u2 = ubuf[SUBLANES - 2:SUBLANES - 2 + ts, :]
    cw = cw_ref[...]
    conv = cw[0:1] * u2 + cw[1:2] * u1 + cw[2:3] * u
    ubuf[0:SUBLANES, :] = u[ts - SUBLANES:, :]
    y = jnp.dot((b_gate * conv).astype(BF16), wout_ref[...], preferred_element_type=F32)
    x1 = x + g1 * y
    x1_ref[0] = x1
    h2_ref[0] = _modulate(x1, cn_ref[...], sh2, sc2)


def _conv_layer(x, mod, tok_norm, ch_norm, w_in, conv_w, w_out):
    b, s, d = x.shape
    ts = min(512, s)
    kern = functools.partial(_conv_layer_kernel, ts=ts)
    const = lambda *shape: pl.BlockSpec(shape, lambda i, j: (0,) * len(shape))
    return pl.pallas_call(
        kern,
        grid=(b, s // ts),
        in_specs=[
            pl.BlockSpec((1, ts, d), lambda i, j: (i, j, 0)),
            pl.BlockSpec((1, 6, d), lambda i, j: (i, 0, 0)),
            const(1, d), const(1, d), const(d, 3 * d), const(3, d), const(d, d),
        ],
        out_specs=[pl.BlockSpec((1, ts, d), lambda i, j: (i, j, 0))] * 2,
        out_shape=[jax.ShapeDtypeStruct((b, s, d), F32)] * 2,
        scratch_shapes=[pltpu.VMEM((ts + SUBLANES, d), F32)],
        compiler_params=_cparams(2),
        name="conv_layer",
    )(x, mod, tok_norm, ch_norm, w_in, conv_w, w_out)


def _mla_proj_kernel(x_ref, mod_ref, tn_ref, pos_ref, freq_ref, sgn_ref,
                     wq_ref, wkv_ref, wkr_ref, qag_ref, kvag_ref,
                     wqbn_ref, wqbr_ref, wkvb_ref, qgn_ref, qgr_ref, kgn_ref, kgr_ref,
                     q_ref, k_ref, v_ref, *, ts):
    x = x_ref[0]
    m = mod_ref[0]
    h = _modulate(x, tn_ref[...], m[0:1], m[1:2]).astype(BF16)
    q_lat = jnp.dot(h, wq_ref[...], preferred_element_type=F32)
    kv_lat = jnp.dot(h, wkv_ref[...], preferred_element_type=F32)
    k_rope = jnp.dot(h, wkr_ref[...], preferred_element_type=F32)[:, :QK_ROPE]

    def rms(v, g):
        return v * lax.rsqrt(jnp.mean(v * v, axis=-1, keepdims=True) + NORM_EPS) * g

    qn = rms(q_lat, qag_ref[...]).astype(BF16)
    kvn = rms(kv_lat, kvag_ref[...]).astype(BF16)
    q_nope = jnp.dot(qn, wqbn_ref[...], preferred_element_type=F32)
    q_rope = jnp.dot(qn, wqbr_ref[...], preferred_element_type=F32)
    kv = jnp.dot(kvn, wkvb_ref[...], preferred_element_type=F32)

    ang = pos_ref[0].astype(F32) * freq_ref[...]
    cos = jnp.cos(ang)
    sin_signed = jnp.sin(ang) * sgn_ref[...]

    def rope(v):
        half = QK_ROPE // 2
        rot = jnp.concatenate([v[:, half:], v[:, :half]], axis=1)
        return v * cos + rot * sin_signed

    scale = QK_HEAD ** -0.5
    kr_g = rope(k_rope * kgr_ref[...])
    kr_ss = jnp.sum(k_rope * k_rope, axis=-1, keepdims=True)
    zpad = jnp.zeros((ts, QK_PAD - QK_HEAD), F32)
    for hd in range(MLA_HEADS):
        qn_h = q_nope[:, hd * QK_NOPE:(hd + 1) * QK_NOPE]
        qr_h = q_rope[:, hd * QK_ROPE:(hd + 1) * QK_ROPE]
        ss = jnp.sum(qn_h * qn_h, axis=-1, keepdims=True) + jnp.sum(qr_h * qr_h, axis=-1, keepdims=True)
        rinv = lax.rsqrt(ss * (1.0 / QK_HEAD) + NORM_EPS) * scale
        qa = qn_h * rinv * qgn_ref[...]
        qb = rope(qr_h * qgr_ref[...]) * rinv
        q_ref[0, hd] = jnp.concatenate([qa, qb, zpad], axis=1).astype(BF16)

        kn_h = kv[:, hd * 2 * QK_NOPE:hd * 2 * QK_NOPE + QK_NOPE]
        v_h = kv[:, hd * 2 * QK_NOPE + QK_NOPE:(hd + 1) * 2 * QK_NOPE]
        kss = jnp.sum(kn_h * kn_h, axis=-1, keepdims=True) + kr_ss
        krinv = lax.rsqrt(kss * (1.0 / QK_HEAD) + NORM_EPS)
        ka = kn_h * krinv * kgn_ref[...]
        kb = kr_g * krinv
        k_ref[0, hd] = jnp.concatenate([ka, kb, zpad], axis=1).astype(BF16)
        v_ref[0, hd] = v_h.astype(BF16)


def _mla_proj(x, mod, tok_norm, pos3, freq, sgn, wq, wkv, wkr, qag, kvag,
              wqbn, wqbr, wkvb, qgn, qgr, kgn, kgr):
    b, s, d = x.shape
    ts = min(512, s)
    kern = functools.partial(_mla_proj_kernel, ts=ts)
    consts = [tok_norm]
    rest = [freq, sgn, wq, wkv, wkr, qag, kvag, wqbn, wqbr, wkvb, qgn, qgr, kgn, kgr]
    cspec = lambda a: pl.BlockSpec(a.shape, lambda i, j: (0,) * a.ndim)
    hq = pl.BlockSpec((1, MLA_HEADS, ts, QK_PAD), lambda i, j: (i, 0, j, 0))
    hv = pl.BlockSpec((1, MLA_HEADS, ts, V_HEAD), lambda i, j: (i, 0, j, 0))
    return pl.pallas_call(
        kern,
        grid=(b, s // ts),
        in_specs=[
            pl.BlockSpec((1, ts, d), lambda i, j: (i, j, 0)),
            pl.BlockSpec((1, 6, d), lambda i, j: (i, 0, 0)),
            cspec(tok_norm),
            pl.BlockSpec((1, ts, 1), lambda i, j: (i, j, 0)),
        ] + [cspec(a) for a in rest],
        out_specs=[hq, hq, hv],
        out_shape=[
            jax.ShapeDtypeStruct((b, MLA_HEADS, s, QK_PAD), BF16),
            jax.ShapeDtypeStruct((b, MLA_HEADS, s, QK_PAD), BF16),
            jax.ShapeDtypeStruct((b, MLA_HEADS, s, V_HEAD), BF16),
        ],
        compiler_params=_cparams(2),
        name="mla_proj",
    )(x, mod, tok_norm, pos3, *rest)


def _flash_kernel(q_ref, k_ref, v_ref, o_ref, *, tq):
    i = pl.program_id(2)
    q = q_ref[0, 0]

    def block(j, carry, diagonal):
        m, l, acc = carry
        start = pl.multiple_of(j * tq, tq)
        kj = k_ref[0, 0, pl.ds(start, tq), :]
        vj = v_ref[0, 0, pl.ds(start, tq), :]
        s = lax.dot_general(q, kj, _NT, preferred_element_type=F32)
        if diagonal:
            row = lax.broadcasted_iota(jnp.int32, (tq, tq), 0)
            col = lax.broadcasted_iota(jnp.int32, (tq, tq), 1)
            s = jnp.where(row >= col, s, NEG_INF)
        m_new = jnp.maximum(m, jnp.max(s, axis=-1, keepdims=True))
        p = jnp.exp(s - m_new)
        alpha = jnp.exp(m - m_new)
        l = alpha * l + jnp.sum(p, axis=-1, keepdims=True)
        acc = alpha * acc + jnp.dot(p.astype(BF16), vj, preferred_element_type=F32)
        return m_new, l, acc

    init = (jnp.full((tq, 1), NEG_INF, F32), jnp.zeros((tq, 1), F32),
            jnp.zeros((tq, V_HEAD), F32))
    carry = lax.fori_loop(0, i, lambda j, c: block(j, c, False), init)
    _, l, acc = block(i, carry, True)
    o_ref[0] = (acc / l).astype(BF16)


def _flash(q, k, v):
    b, hds, s, _ = q.shape
    tq = min(512, s)
    kern = functools.partial(_flash_kernel, tq=tq)
    return pl.pallas_call(
        kern,
        grid=(b, hds, s // tq),
        in_specs=[
            pl.BlockSpec((1, 1, tq, QK_PAD), lambda bi, h, i: (bi, h, i, 0)),
            pl.BlockSpec((1, 1, s, QK_PAD), lambda bi, h, i: (bi, h, 0, 0)),
            pl.BlockSpec((1, 1, s, V_HEAD), lambda bi, h, i: (bi, h, 0, 0)),
        ],
        out_specs=pl.BlockSpec((1, tq, V_HEAD), lambda bi, h, i: (bi, i, h)),
        out_shape=jax.ShapeDtypeStruct((b, s, hds * V_HEAD), BF16),
        compiler_params=_cparams(3),
        name="mla_flash",
    )(q, k, v)


def _mla_out_kernel(o_ref, x_ref, mod_ref, cn_ref, w_ref, x1_ref, h2_ref):
    m = mod_ref[0]
    y = jnp.dot(o_ref[0], w_ref[...], preferred_element_type=F32)
    x1 = x_ref[0] + m[2:3] * y
    x1_ref[0] = x1
    h2_ref[0] = _modulate(x1, cn_ref[...], m[3:4], m[4:5])


def _mla_out(o, x, mod, ch_norm, w_out):
    b, s, d = x.shape
    ts = min(512, s)
    return pl.pallas_call(
        _mla_out_kernel,
        grid=(b, s // ts),
        in_specs=[
            pl.BlockSpec((1, ts, d), lambda i, j: (i, j, 0)),
            pl.BlockSpec((1, ts, d), lambda i, j: (i, j, 0)),
            pl.BlockSpec((1, 6, d), lambda i, j: (i, 0, 0)),
            pl.BlockSpec((1, d), lambda i, j: (0, 0)),
            pl.BlockSpec((d, d), lambda i, j: (0, 0)),
        ],
        out_specs=[pl.BlockSpec((1, ts, d), lambda i, j: (i, j, 0))] * 2,
        out_shape=[jax.ShapeDtypeStruct((b, s, d), F32)] * 2,
        compiler_params=_cparams(2),
        name="mla_out",
    )(o, x, mod, ch_norm, w_out)


def _acc_row(acc, iota, r, row):
    return jnp.where(iota == r, row, acc)


def _top16_keys(s):
    n, l = s.shape
    iota = lax.broadcasted_iota(jnp.int32, (n, l), 0)
    iota_k = lax.broadcasted_iota(jnp.int32, (PEER_TOPK, l), 0)
    vals = jnp.zeros((PEER_TOPK, l), F32)
    idxs = jnp.zeros((PEER_TOPK, l), jnp.int32)
    for r in range(PEER_TOPK):
        mx = jnp.max(s, axis=0, keepdims=True)
        idx = jnp.min(jnp.where(s == mx, iota, n), axis=0, keepdims=True)
        vals = _acc_row(vals, iota_k, r, mx)
        idxs = _acc_row(idxs, iota_k, r, idx)
        s = jnp.where(iota == idx, NEG_INF, s)
    return vals, idxs


def _pair_candidates(s0, i0, s1, i1):
    l = s0.shape[1]
    row8 = lax.broadcasted_iota(jnp.int32, (SUBLANES, l), 0)
    sc, ca, cb = [], [], []
    sc.append(s0[0:1] + s1)
    ca.append(jnp.broadcast_to(i0[0:1], (PEER_TOPK, l)))
    cb.append(i1)
    for i in range(1, SUBLANES):
        nj = PEER_TOPK // (i + 1)
        sc.append(jnp.where(row8 < nj, s0[i:i + 1] + s1[0:SUBLANES], NEG_INF))
        ca.append(jnp.broadcast_to(i0[i:i + 1], (SUBLANES, l)))
        cb.append(i1[0:SUBLANES])
    sc.append(s0[SUBLANES:] + s1[0:1])
    ca.append(i0[SUBLANES:])
    cb.append(jnp.broadcast_to(i1[0:1], (SUBLANES, l)))
    return (jnp.concatenate(sc, axis=0), jnp.concatenate(ca, axis=0),
            jnp.concatenate(cb, axis=0))


def _top16_pairs(sc, ca, cb):
    n, l = sc.shape
    iota = lax.broadcasted_iota(jnp.int32, (n, l), 0)
    iota_k = lax.broadcasted_iota(jnp.int32, (PEER_TOPK, l), 0)
    vals = jnp.zeros((PEER_TOPK, l), F32)
    ia = jnp.zeros((PEER_TOPK, l), jnp.int32)
    ib = jnp.zeros((PEER_TOPK, l), jnp.int32)
    for r in range(PEER_TOPK):
        mx = jnp.max(sc, axis=0, keepdims=True)
        idx = jnp.min(jnp.where(sc == mx, iota, n), axis=0, keepdims=True)
        hit = iota == idx
        vals = _acc_row(vals, iota_k, r, mx)
        ia = _acc_row(ia, iota_k, r, jnp.max(jnp.where(hit, ca, -1), axis=0, keepdims=True))
        ib = _acc_row(ib, iota_k, r, jnp.max(jnp.where(hit, cb, -1), axis=0, keepdims=True))
        sc = jnp.where(hit, NEG_INF, sc)
    return vals, ia, ib


def _peer_route_kernel(h_ref, wqh_ref, wql_ref, skh_ref, skl_ref, w_ref,
                       q_s, iat_s, ibt_s, gt_s, ia_s, ib_s, ghi_s, glo_s, *, tt):
    h_hi, h_lo = _split_bf16(h_ref[...])
    q_s[...] = _dot3(h_hi, h_lo, wqh_ref[...], wql_ref[...])

    def head_body(hd, carry):
        tops = []
        for p in range(2):
            col = pl.multiple_of(hd * (2 * PEER_HALF) + p * PEER_HALF, PEER_HALF)
            q_hi, q_lo = _split_bf16(q_s[:, pl.ds(col, PEER_HALF)])
            s = _dot3(skh_ref[hd, p], skl_ref[hd, p], q_hi, q_lo, _NT)
            tops.append(_top16_keys(s))
        (s0, i0), (s1, i1) = tops
        best, ia, ib = _top16_pairs(*_pair_candidates(s0, i0, s1, i1))
        e = jnp.exp(best - best[0:1])
        g = e / jnp.sum(e, axis=0, keepdims=True)
        rows = pl.ds(pl.multiple_of(hd * PEER_TOPK, PEER_TOPK), PEER_TOPK)
        iat_s[rows, :] = ia.astype(F32)
        ibt_s[rows, :] = ib.astype(F32)
        gt_s[rows, :] = g
        return carry

    lax.fori_loop(0, PEER_HEADS, head_body, 0)

    ia_s[...] = iat_s[...].T
    ib_s[...] = ibt_s[...].T
    g = gt_s[...].T
    g_hi = g.astype(BF16).astype(F32)
    ghi_s[...] = g_hi
    glo_s[...] = g - g_hi

    key = lax.broadcasted_iota(jnp.int32, (PEER_KEYS, LANES), 0).astype(F32)

    def token_body(t, carry):
        row = pl.ds(t, 1)
        hit_a = ia_s[row, :] == key
        hit_b = ib_s[row, :] == key
        a_hi = jnp.where(hit_a, ghi_s[row, :], 0.0)
        a_lo = jnp.where(hit_a, glo_s[row, :], 0.0)
        b_one = jnp.where(hit_b, 1.0, 0.0)
        lhs = jnp.concatenate([a_hi, a_lo], axis=1).astype(BF16)
        rhs = jnp.concatenate([b_one, b_one], axis=1).astype(BF16)
        w_ref[t] = lax.dot_general(lhs, rhs, _NT, preferred_element_type=F32)
        return carry

    lax.fori_loop(0, tt, token_body, 0, unroll=4)


def _peer_route(h2, wq_hi, wq_lo, sk_hi, sk_lo):
    t, d = h2.shape
    tt = 128
    nq = wq_hi.shape[1]
    kern = functools.partial(_peer_route_kernel, tt=tt)
    cspec = lambda a: pl.BlockSpec(a.shape, lambda i: (0,) * a.ndim)
    slab = pltpu.VMEM((PEER_HEADS * PEER_TOPK, tt), F32)
    tok = pltpu.VMEM((tt, PEER_HEADS * PEER_TOPK), F32)
    return pl.pallas_call(
        kern,
        grid=(t // tt,),
        in_specs=[pl.BlockSpec((tt, d), lambda i: (i, 0)),
                  cspec(wq_hi), cspec(wq_lo), cspec(sk_hi), cspec(sk_lo)],
        out_specs=pl.BlockSpec((tt, PEER_KEYS, PEER_KEYS), lambda i: (i, 0, 0)),
        out_shape=jax.ShapeDtypeStruct((t, PEER_KEYS, PEER_KEYS), F32),
        scratch_shapes=[pltpu.VMEM((tt, nq), F32), slab, slab, slab, tok, tok, tok, tok],
        compiler_params=_cparams(1),
        name="peer_route",
    )(h2, wq_hi, wq_lo, sk_hi, sk_lo)


def _gelu(z):
    return 0.5 * z * (1.0 + lax.erf(z * (1.0 / math.sqrt(2.0))))


def _peer_dense_kernel(h_ref, w_ref, u_ref, v_ref, x_ref, mod_ref, o_ref, acc_ref, *, n_ia):
    j = pl.program_id(1)

    @pl.when(j == 0)
    def _():
        acc_ref[...] = jnp.zeros_like(acc_ref)

    z = lax.dot_general(h_ref[...].astype(BF16), u_ref[...], _NT, preferred_element_type=F32)
    w = jnp.concatenate([w_ref[:, i, :] for i in range(n_ia)], axis=1)
    a = (_gelu(z) * w).astype(BF16)
    acc_ref[...] += jnp.dot(a, v_ref[...], preferred_element_type=F32)

    @pl.when(j == pl.num_programs(1) - 1)
    def _():
        o_ref[...] = x_ref[...] + mod_ref[0][5:6] * acc_ref[...]


def _peer_dense(h2, w3, u_bf, v_bf, x, mod, seq):
    t, d = h2.shape
    tt = min(512, seq)
    n_ia = SUBLANES
    e_blk = n_ia * PEER_KEYS
    kern = functools.partial(_peer_dense_kernel, n_ia=n_ia)
    return pl.pallas_call(
        kern,
        grid=(t // tt, PEER_EXPERTS // e_blk),
        in_specs=[
            pl.BlockSpec((tt, d), lambda i, j: (i, 0)),
            pl.BlockSpec((tt, n_ia, PEER_KEYS), lambda i, j: (i, j, 0)),
            pl.BlockSpec((e_blk, d), lambda i, j: (j, 0)),
            pl.BlockSpec((e_blk, d), lambda i, j: (j, 0)),
            pl.BlockSpec((tt, d), lambda i, j: (i, 0)),
            pl.BlockSpec((1, 6, d), lambda i, j: ((i * tt) // seq, 0, 0)),
        ],
        out_specs=pl.BlockSpec((tt, d), lambda i, j: (i, 0)),
        out_shape=jax.ShapeDtypeStruct((t, d), F32),
        scratch_shapes=[pltpu.VMEM((tt, d), F32)],
        compiler_params=_cparams(2),
        name="peer_dense",
    )(h2, w3, u_bf, v_bf, x, mod)


def _peer(h2, x, mod, w_q, subkeys, u_tab, v_tab):
    b, s, d = x.shape
    h2f = h2.reshape(b * s, d)
    wq_hi, wq_lo = _split_bf16(w_q)
    sk_hi, sk_lo = _split_bf16(subkeys)
    w3 = _peer_route(h2f, wq_hi, wq_lo, sk_hi, sk_lo)
    out = _peer_dense(h2f, w3, u_tab.astype(BF16), v_tab.astype(BF16),
                      x.reshape(b * s, d), mod, s)
    return out.reshape(b, s, d)


def kernel(x, c, positions, tok_norm, ch_norm, ada_w, ada_b, conv_in_w, conv_w, conv_out_w,
           mla_in_w, q_a_norm, kv_a_norm, q_b_w, kv_b_w, q_norm, k_norm, mla_out_w,
           peer_q_w, peer_subkeys, peer_u, peer_v):
    b, s, d = x.shape
    depth = ada_w.shape[0]
    bp = -(-b // SUBLANES) * SUBLANES
    c_pad = jnp.zeros((bp, d), F32).at[:b].set(c)
    mod_all = _ada(c_pad, ada_w, ada_b)[:, :b].reshape(depth, b, 6, d)

    row = lambda a: a.reshape(1, -1)
    for i in range(depth):
        mod = mod_all[i]
        j = i // 2
        if i % 2 == 0:
            x, h2 = _conv_layer(x, mod, row(tok_norm[i]), row(ch_norm[i]),
                                conv_in_w[j].astype(BF16), conv_w[j],
                                conv_out_w[j].astype(BF16))
        else:
            w_in = mla_in_w[j].astype(BF16)
            wq = w_in[:, :Q_LORA]
            wkv = w_in[:, Q_LORA:Q_LORA + KV_LORA]
            wkr = jnp.pad(w_in[:, Q_LORA + KV_LORA:], ((0, 0), (0, LANES - QK_ROPE)))
            wqb = q_b_w[j].astype(BF16).reshape(Q_LORA, MLA_HEADS, QK_HEAD)
            wqbn = wqb[:, :, :QK_NOPE].reshape(Q_LORA, MLA_HEADS * QK_NOPE)
            wqbr = wqb[:, :, QK_NOPE:].reshape(Q_LORA, MLA_HEADS * QK_ROPE)
            inv_freq = ROPE_THETA ** (-jnp.arange(0, QK_ROPE, 2, dtype=F32) / QK_ROPE)
            freq = jnp.concatenate([inv_freq, inv_freq]).reshape(1, QK_ROPE)
            half = QK_ROPE // 2
            sgn = jnp.concatenate([-jnp.ones((half,), F32), jnp.ones((half,), F32)]).reshape(1, QK_ROPE)
            q, k, v = _mla_proj(
                x, mod, row(tok_norm[i]), positions.reshape(b, s, 1), freq, sgn,
                wq, wkv, wkr, row(q_a_norm[j]), row(kv_a_norm[j]),
                wqbn, wqbr, kv_b_w[j].astype(BF16),
                row(q_norm[j][:QK_NOPE]), row(q_norm[j][QK_NOPE:]),
                row(k_norm[j][:QK_NOPE]), row(k_norm[j][QK_NOPE:]))
            o = _flash(q, k, v)
            x, h2 = _mla_out(o, x, mod, row(ch_norm[i]), mla_out_w[j].astype(BF16))
        x = _peer(h2, x, mod, peer_q_w[i], peer_subkeys[i], peer_u[i], peer_v[i])
    return x
```

```python
import functools
import math

import jax
import jax.numpy as jnp
from jax import lax
from jax.experimental import pallas as pl
from jax.experimental.pallas import tpu as pltpu

F32 = jnp.float32
BF16 = jnp.bfloat16

D_MODEL = 1024
MLA_HEADS = 8
Q_LORA = 384
KV_LORA = 256
QK_NOPE = 128
QK_ROPE = 64
QK_HEAD = QK_NOPE + QK_ROPE
QK_PAD = 256
V_HEAD = 128
ROPE_THETA = 10000.0
PEER_HEADS = 8
PEER_KEYS = 128
PEER_EXPERTS = PEER_KEYS * PEER_KEYS
PEER_HALF = 128
PEER_TOPK = 16
NORM_EPS = 1e-6

LANES = 128
SUBLANES = 8
VMEM_LIMIT = 56 * 1024 * 1024

NEG_INF = float("-inf")
_NT = (((1,), (1,)), ((), ()))


def _cparams(n_axes):
    return pltpu.CompilerParams(
        dimension_semantics=("arbitrary",) * n_axes, vmem_limit_bytes=VMEM_LIMIT)


def _split_bf16(a):
    hi = a.astype(BF16)
    lo = (a - hi.astype(F32)).astype(BF16)
    return hi, lo


def _dot3(a_hi, a_lo, b_hi, b_lo, dims=None):
    if dims is None:
        f = lambda a, b: jnp.dot(a, b, preferred_element_type=F32)
    else:
        f = lambda a, b: lax.dot_general(a, b, dims, preferred_element_type=F32)
    return f(a_hi, b_hi) + (f(a_hi, b_lo) + f(a_lo, b_hi))


def _modulate(x, g, shift, scale):
    ms = jnp.mean(x * x, axis=-1, keepdims=True)
    y = x * lax.rsqrt(ms + NORM_EPS)
    return (y * g) * (1.0 + scale) + shift


def _ada_kernel(c_ref, w_ref, b_ref, o_ref):
    c = c_ref[...]
    ca = c * jax.nn.sigmoid(c)
    o_ref[0] = jnp.dot(ca, w_ref[0], preferred_element_type=F32,
                       precision=lax.Precision.HIGHEST) + b_ref[0]


def _ada(c_pad, ada_w, ada_b):
    depth, d, d6 = ada_w.shape
    bp = c_pad.shape[0]
    nb = d6 // d
    return pl.pallas_call(
        _ada_kernel,
        grid=(depth, nb),
        in_specs=[
            pl.BlockSpec((bp, d), lambda i, j: (0, 0)),
            pl.BlockSpec((1, d, d), lambda i, j: (i, 0, j)),
            pl.BlockSpec((1, 1, d), lambda i, j: (i, 0, j)),
        ],
        out_specs=pl.BlockSpec((1, bp, d), lambda i, j: (i, 0, j)),
        out_shape=jax.ShapeDtypeStruct((depth, bp, d6), F32),
        compiler_params=_cparams(2),
        name="ada",
    )(c_pad, ada_w, ada_b.reshape(depth, 1, d6))


def _conv_layer_kernel(x_ref, mod_ref, tn_ref, cn_ref, win_ref, cw_ref, wout_ref,
                       x1_ref, h2_ref, ubuf, *, ts):
    d = D_MODEL
    x = x_ref[0]
    m = mod_ref[0]
    sh1, sc1, g1, sh2, sc2 = m[0:1], m[1:2], m[2:3], m[3:4], m[4:5]
    h = _modulate(x, tn_ref[...], sh1, sc1)
    bcv = jnp.dot(h.astype(BF16), win_ref[...], preferred_element_type=F32)
    b_gate = bcv[:, :d]
    u = bcv[:, d:2 * d] * bcv[:, 2 * d:]

    @pl.when(pl.program_id(1) == 0)
    def _():
        ubuf[0:SUBLANES, :] = jnp.zeros((SUBLANES, d), F32)

    ubuf[SUBLANES:SUBLANES + ts, :] = u
    u1 = ubuf[SUBLANES - 1:SUBLANES - 1 + ts, :]
    u2 = ubuf[SUBLANES - 2:SUBLANES - 2 + ts, :]
    cw = cw_ref[...]
    conv = cw[0:1] * u2 + cw[1:2] * u1 + cw[2:3] * u
    ubuf[0:SUBLANES, :] = u[ts - SUBLANES:, :]
    y = jnp.dot((b_gate * conv).astype(BF16), wout_ref[...], preferred_element_type=F32)
    x1 = x + g1 * y
    x1_ref[0] = x1
    h2_ref[0] = _modulate(x1, cn_ref[...], sh2, sc2)


def _conv_layer(x, mod, tok_norm, ch_norm, w_in, conv_w, w_out):
    b, s, d = x.shape
    ts = min(512, s)
    kern = functools.partial(_conv_layer_kernel, ts=ts)
    const = lambda *shape: pl.BlockSpec(shape, lambda i, j: (0,) * len(shape))
    return pl.pallas_call(
        kern,
        grid=(b, s // ts),
        in_specs=[
            pl.BlockSpec((1, ts, d), lambda i, j: (i, j, 0)),
            pl.BlockSpec((1, 6, d), lambda i, j: (i, 0, 0)),
            const(1, d), const(1, d), const(d, 3 * d), const(3, d), const(d, d),
        ],
        out_specs=[pl.BlockSpec((1, ts, d), lambda i, j: (i, j, 0))] * 2,
        out_shape=[jax.ShapeDtypeStruct((b, s, d), F32)] * 2,
        scratch_shapes=[pltpu.VMEM((ts + SUBLANES, d), F32)],
        compiler_params=_cparams(2),
        name="conv_layer",
    )(x, mod, tok_norm, ch_norm, w_in, conv_w, w_out)


def _mla_proj_kernel(x_ref, mod_ref, tn_ref, pos_ref, freq_ref, sgn_ref,
                     wq_ref, wkv_ref, wkr_ref, qag_ref, kvag_ref,
                     wqbn_ref, wqbr_ref, wkvb_ref, qgn_ref, qgr_ref, kgn_ref, kgr_ref,
                     q_ref, k_ref, v_ref, *, ts):
    x = x_ref[0]
    m = mod_ref[0]
    h = _modulate(x, tn_ref[...], m[0:1], m[1:2]).astype(BF16)
    q_lat = jnp.dot(h, wq_ref[...], preferred_element_type=F32)
    kv_lat = jnp.dot(h, wkv_ref[...], preferred_element_type=F32)
    k_rope = jnp.dot(h, wkr_ref[...], preferred_element_type=F32)[:, :QK_ROPE]

    def rms(v, g):
        return v * lax.rsqrt(jnp.mean(v * v, axis=-1, keepdims=True) + NORM_EPS) * g

    qn = rms(q_lat, qag_ref[...]).astype(BF16)
    kvn = rms(kv_lat, kvag_ref[...]).astype(BF16)
    q_nope = jnp.dot(qn, wqbn_ref[...], preferred_element_type=F32)
    q_rope = jnp.dot(qn, wqbr_ref[...], preferred_element_type=F32)
    kv = jnp.dot(kvn, wkvb_ref[...], preferred_element_type=F32)

    ang = pos_ref[0].astype(F32) * freq_ref[...]
    cos = jnp.cos(ang)
    sin_signed = jnp.sin(ang) * sgn_ref[...]

    def rope(v):
        half = QK_ROPE // 2
        rot = jnp.concatenate([v[:, half:], v[:, :half]], axis=1)
        return v * cos + rot * sin_signed

    scale = QK_HEAD ** -0.5
    kr_g = rope(k_rope * kgr_ref[...])
    kr_ss = jnp.sum(k_rope * k_rope, axis=-1, keepdims=True)
    zpad = jnp.zeros((ts, QK_PAD - QK_HEAD), F32)
    for hd in range(MLA_HEADS):
        qn_h = q_nope[:, hd * QK_NOPE:(hd + 1) * QK_NOPE]
        qr_h = q_rope[:, hd * QK_ROPE:(hd + 1) * QK_ROPE]
        ss = jnp.sum(qn_h * qn_h, axis=-1, keepdims=True) + jnp.sum(qr_h * qr_h, axis=-1, keepdims=True)
        rinv = lax.rsqrt(ss * (1.0 / QK_HEAD) + NORM_EPS) * scale
        qa = qn_h * rinv * qgn_ref[...]
        qb = rope(qr_h * qgr_ref[...]) * rinv
        q_ref[0, hd] = jnp.concatenate([qa, qb, zpad], axis=1).astype(BF16)

        kn_h = kv[:, hd * 2 * QK_NOPE:hd * 2 * QK_NOPE + QK_NOPE]
        v_h = kv[:, hd * 2 * QK_NOPE + QK_NOPE:(hd + 1) * 2 * QK_NOPE]
        kss = jnp.sum(kn_h * kn_h, axis=-1, keepdims=True) + kr_ss
        krinv = lax.rsqrt(kss * (1.0 / QK_HEAD) + NORM_EPS)
        ka = kn_h * krinv * kgn_ref[...]
        kb = kr_g * krinv
        k_ref[0, hd] = jnp.concatenate([ka, kb, zpad], axis=1).astype(BF16)
        v_ref[0, hd] = v_h.astype(BF16)


def _mla_proj(x, mod, tok_norm, pos3, freq, sgn, wq, wkv, wkr, qag, kvag,
              wqbn, wqbr, wkvb, qgn, qgr, kgn, kgr):
    b, s, d = x.shape
    ts = min(512, s)
    kern = functools.partial(_mla_proj_kernel, ts=ts)
    consts = [tok_norm]
    rest = [freq, sgn, wq, wkv, wkr, qag, kvag, wqbn, wqbr, wkvb, qgn, qgr, kgn, kgr]
    cspec = lambda a: pl.BlockSpec(a.shape, lambda i, j: (0,) * a.ndim)
    hq = pl.BlockSpec((1, MLA_HEADS, ts, QK_PAD), lambda i, j: (i, 0, j, 0))
    hv = pl.BlockSpec((1, MLA_HEADS, ts, V_HEAD), lambda i, j: (i, 0, j, 0))
    return pl.pallas_call(
        kern,
        grid=(b, s // ts),
        in_specs=[
            pl.BlockSpec((1, ts, d), lambda i, j: (i, j, 0)),
            pl.BlockSpec((1, 6, d), lambda i, j: (i, 0, 0)),
            cspec(tok_norm),
            pl.BlockSpec((1, ts, 1), lambda i, j: (i, j, 0)),
        ] + [cspec(a) for a in rest],
        out_specs=[hq, hq, hv],
        out_shape=[
            jax.ShapeDtypeStruct((b, MLA_HEADS, s, QK_PAD), BF16),
            jax.ShapeDtypeStruct((b, MLA_HEADS, s, QK_PAD), BF16),
            jax.ShapeDtypeStruct((b, MLA_HEADS, s, V_HEAD), BF16),
        ],
        compiler_params=_cparams(2),
        name="mla_proj",
    )(x, mod, tok_norm, pos3, *rest)


FLASH_HEADS = 2


def _flash_kernel(q_ref, k_ref, v_ref, o_ref, *, tq):
    i = pl.program_id(2)

    def block(j, carries, diagonal):
        start = pl.multiple_of(j * tq, tq)
        out = []
        for hh in range(FLASH_HEADS):
            m, l, acc = carries[hh]
            kj = k_ref[0, hh, pl.ds(start, tq), :]
            vj = v_ref[0, hh, pl.ds(start, tq), :]
            s = lax.dot_general(q_ref[0, hh], kj, _NT, preferred_element_type=F32)
            if diagonal:
                row = lax.broadcasted_iota(jnp.int32, (tq, tq), 0)
                col = lax.broadcasted_iota(jnp.int32, (tq, tq), 1)
                s = jnp.where(row >= col, s, NEG_INF)
            m_new = jnp.maximum(m, jnp.max(s, axis=-1, keepdims=True))
            p = jnp.exp(s - m_new)
            alpha = jnp.exp(m - m_new)
            l = alpha * l + jnp.sum(p, axis=-1, keepdims=True)
            acc = alpha * acc + jnp.dot(p.astype(BF16), vj, preferred_element_type=F32)
            out.append((m_new, l, acc))
        return tuple(out)

    init = tuple((jnp.full((tq, 1), NEG_INF, F32), jnp.zeros((tq, 1), F32),
                  jnp.zeros((tq, V_HEAD), F32)) for _ in range(FLASH_HEADS))
    carries = lax.fori_loop(0, i, lambda j, c: block(j, c, False), init)
    carries = block(i, carries, True)
    o_ref[0] = jnp.concatenate([(acc / l).astype(BF16) for _, l, acc in carries], axis=1)


def _flash(q, k, v):
    b, hds, s, _ = q.shape
    tq = min(512, s)
    kern = functools.partial(_flash_kernel, tq=tq)
    fh = FLASH_HEADS
    return pl.pallas_call(
        kern,
        grid=(b, hds // fh, s // tq),
        in_specs=[
            pl.BlockSpec((1, fh, tq, QK_PAD), lambda bi, h, i: (bi, h, i, 0)),
            pl.BlockSpec((1, fh, s, QK_PAD), lambda bi, h, i: (bi, h, 0, 0)),
            pl.BlockSpec((1, fh, s, V_HEAD), lambda bi, h, i: (bi, h, 0, 0)),
        ],
        out_specs=pl.BlockSpec((1, tq, fh * V_HEAD), lambda bi, h, i: (bi, i, h)),
        out_shape=jax.ShapeDtypeStruct((b, s, hds * V_HEAD), BF16),
        compiler_params=_cparams(3),
        name="mla_flash",
    )(q, k, v)


def _mla_out_kernel(o_ref, x_ref, mod_ref, cn_ref, w_ref, x1_ref, h2_ref):
    m = mod_ref[0]
    y = jnp.dot(o_ref[0], w_ref[...], preferred_element_type=F32)
    x1 = x_ref[0] + m[2:3] * y
    x1_ref[0] = x1
    h2_ref[0] = _modulate(x1, cn_ref[...], m[3:4], m[4:5])


def _mla_out(o, x, mod, ch_norm, w_out):
    b, s, d = x.shape
    ts = min(512, s)
    return pl.pallas_call(
        _mla_out_kernel,
        grid=(b, s // ts),
        in_specs=[
            pl.BlockSpec((1, ts, d), lambda i, j: (i, j, 0)),
            pl.BlockSpec((1, ts, d), lambda i, j: (i, j, 0)),
            pl.BlockSpec((1, 6, d), lambda i, j: (i, 0, 0)),
            pl.BlockSpec((1, d), lambda i, j: (0, 0)),
            pl.BlockSpec((d, d), lambda i, j: (0, 0)),
        ],
        out_specs=[pl.BlockSpec((1, ts, d), lambda i, j: (i, j, 0))] * 2,
        out_shape=[jax.ShapeDtypeStruct((b, s, d), F32)] * 2,
        compiler_params=_cparams(2),
        name="mla_out",
    )(o, x, mod, ch_norm, w_out)


def _acc_row(acc, iota, r, row):
    return jnp.where(iota == r, row, acc)


def _top16_keys(s, key, rank):
    n, l = s.shape
    vals = jnp.zeros((PEER_TOPK, l), F32)
    idxs = jnp.zeros((PEER_TOPK, l), F32)
    for r in range(PEER_TOPK):
        mx = jnp.max(s, axis=0, keepdims=True)
        idx = jnp.min(jnp.where(s == mx, key, float(n)), axis=0, keepdims=True)
        vals = _acc_row(vals, rank, float(r), mx)
        idxs = _acc_row(idxs, rank, float(r), idx)
        s = jnp.where(key == idx, NEG_INF, s)
    return vals, idxs


def _pair_candidates(s0, i0, s1, i1):
    l = s0.shape[1]
    row8 = lax.broadcasted_iota(jnp.int32, (SUBLANES, l), 0)
    e0 = i0 * float(PEER_KEYS)
    sc, ce = [], []
    sc.append(s0[0:1] + s1)
    ce.append(e0[0:1] + i1)
    for i in range(1, SUBLANES):
        nj = PEER_TOPK // (i + 1)
        sc.append(jnp.where(row8 < nj, s0[i:i + 1] + s1[0:SUBLANES], NEG_INF))
        ce.append(e0[i:i + 1] + i1[0:SUBLANES])
    sc.append(s0[SUBLANES:] + s1[0:1])
    ce.append(e0[SUBLANES:] + i1[0:1])
    return jnp.concatenate(sc, axis=0), jnp.concatenate(ce, axis=0)


def _top16_pairs(sc, ce, rank):
    n, l = sc.shape
    pos = lax.broadcasted_iota(jnp.int32, (n, l), 0).astype(F32)
    vals = jnp.zeros((PEER_TOPK, l), F32)
    exps = jnp.zeros((PEER_TOPK, l), F32)
    for r in range(PEER_TOPK):
        mx = jnp.max(sc, axis=0, keepdims=True)
        idx = jnp.min(jnp.where(sc == mx, pos, float(n)), axis=0, keepdims=True)
        hit = pos == idx
        vals = _acc_row(vals, rank, float(r), mx)
        exps = _acc_row(exps, rank, float(r),
                        jnp.max(jnp.where(hit, ce, -1.0), axis=0, keepdims=True))
        sc = jnp.where(hit, NEG_INF, sc)
    return vals, exps


HEADS_PER_STEP = 8


def _peer_route_kernel(h_ref, wqh_ref, wql_ref, skh_ref, skl_ref, w_ref,
                       q_s, iat_s, ibt_s, gt_s, ia_s, ib_s, ghi_s, glo_s, *, tt):
    h_hi, h_lo = _split_bf16(h_ref[...])
    q_s[...] = _dot3(h_hi, h_lo, wqh_ref[...], wql_ref[...])

    def one_head(hd, key, rank):
        tops = []
        for p in range(2):
            col = pl.multiple_of(hd * (2 * PEER_HALF) + p * PEER_HALF, PEER_HALF)
            q_hi, q_lo = _split_bf16(q_s[:, pl.ds(col, PEER_HALF)])
            s = _dot3(skh_ref[hd, p], skl_ref[hd, p], q_hi, q_lo, _NT)
            tops.append(_top16_keys(s, key, rank))
        (s0, i0), (s1, i1) = tops
        best, expert = _top16_pairs(*_pair_candidates(s0, i0, s1, i1), rank)
        e = jnp.exp(best - best[0:1])
        g = e / jnp.sum(e, axis=0, keepdims=True)
        ia = jnp.floor(expert * (1.0 / PEER_KEYS))
        rows = pl.ds(pl.multiple_of(hd * PEER_TOPK, PEER_TOPK), PEER_TOPK)
        iat_s[rows, :] = ia
        ibt_s[rows, :] = expert - ia * float(PEER_KEYS)
        gt_s[rows, :] = g

    def head_body(step, carry):
        key = lax.broadcasted_iota(jnp.int32, (PEER_KEYS, tt), 0).astype(F32)
        rank = lax.broadcasted_iota(jnp.int32, (PEER_TOPK, tt), 0).astype(F32)
        for u in range(HEADS_PER_STEP):
            one_head(step * HEADS_PER_STEP + u, key, rank)
        return carry

    lax.fori_loop(0, PEER_HEADS // HEADS_PER_STEP, head_body, 0)

    ia_s[...] = iat_s[...].T
    ib_s[...] = ibt_s[...].T
    g = gt_s[...].T
    g_hi = g.astype(BF16).astype(F32)
    ghi_s[...] = g_hi
    glo_s[...] = g - g_hi

    key = lax.broadcasted_iota(jnp.int32, (PEER_KEYS, LANES), 0).astype(F32)

    def token_body(t, carry):
        row = pl.ds(t, 1)
        hit_a = ia_s[row, :] == key
        hit_b = ib_s[row, :] == key
        a_hi = jnp.where(hit_a, ghi_s[row, :], 0.0)
        a_lo = jnp.where(hit_a, glo_s[row, :], 0.0)
        b_one = jnp.where(hit_b, 1.0, 0.0)
        lhs = jnp.concatenate([a_hi, a_lo], axis=1).astype(BF16)
        rhs = jnp.concatenate([b_one, b_one], axis=1).astype(BF16)
        w_ref[t] = lax.dot_general(lhs, rhs, _NT, preferred_element_type=F32)
        return carry

    lax.fori_loop(0, tt, token_body, 0, unroll=16)


def _peer_route(h2, wq_hi, wq_lo, sk_hi, sk_lo):
    t, d = h2.shape
    tt = 128
    nq = wq_hi.shape[1]
    kern = functools.partial(_peer_route_kernel, tt=tt)
    cspec = lambda a: pl.BlockSpec(a.shape, lambda i: (0,) * a.ndim)
    slab = pltpu.VMEM((PEER_HEADS * PEER_TOPK, tt), F32)
    tok = pltpu.VMEM((tt, PEER_HEADS * PEER_TOPK), F32)
    return pl.pallas_call(
        kern,
        grid=(t // tt,),
        in_specs=[pl.BlockSpec((tt, d), lambda i: (i, 0)),
                  cspec(wq_hi), cspec(wq_lo), cspec(sk_hi), cspec(sk_lo)],
        out_specs=pl.BlockSpec((tt, PEER_KEYS, PEER_KEYS), lambda i: (i, 0, 0)),
        out_shape=jax.ShapeDtypeStruct((t, PEER_KEYS, PEER_KEYS), F32),
        scratch_shapes=[pltpu.VMEM((tt, nq), F32), slab, slab, slab, tok, tok, tok, tok],
        compiler_params=_cparams(1),
        name="peer_route",
    )(h2, wq_hi, wq_lo, sk_hi, sk_lo)


def _gelu(z):
    return 0.5 * z * (1.0 + lax.erf(z * (1.0 / math.sqrt(2.0))))


def _peer_dense_kernel(h_ref, w_ref, u_ref, v_ref, x_ref, mod_ref, o_ref, acc_ref, *, n_ia):
    j = pl.program_id(1)

    @pl.when(j == 0)
    def _():
        acc_ref[...] = jnp.zeros_like(acc_ref)

    z = lax.dot_general(h_ref[...].astype(BF16), u_ref[...], _NT, preferred_element_type=F32)
    w = jnp.concatenate([w_ref[:, i, :] for i in range(n_ia)], axis=1)
    a = (_gelu(z) * w).astype(BF16)
    acc_ref[...] += jnp.dot(a, v_ref[...], preferred_element_type=F32)

    @pl.when(j == pl.num_programs(1) - 1)
    def _():
        o_ref[...] = x_ref[...] + mod_ref[0][5:6] * acc_ref[...]


def _peer_dense(h2, w3, u_bf, v_bf, x, mod, seq):
    t, d = h2.shape
    tt = min(512, seq)
    n_ia = SUBLANES
    e_blk = n_ia * PEER_KEYS
    kern = functools.partial(_peer_dense_kernel, n_ia=n_ia)
    return pl.pallas_call(
        kern,
        grid=(t // tt, PEER_EXPERTS // e_blk),
        in_specs=[
            pl.BlockSpec((tt, d), lambda i, j: (i, 0)),
            pl.BlockSpec((tt, n_ia, PEER_KEYS), lambda i, j: (i, j, 0)),
            pl.BlockSpec((e_blk, d), lambda i, j: (j, 0)),
            pl.BlockSpec((e_blk, d), lambda i, j: (j, 0)),
            pl.BlockSpec((tt, d), lambda i, j: (i, 0)),
            pl.BlockSpec((1, 6, d), lambda i, j: ((i * tt) // seq, 0, 0)),
        ],
        out_specs=pl.BlockSpec((tt, d), lambda i, j: (i, 0)),
        out_shape=jax.ShapeDtypeStruct((t, d), F32),
        scratch_shapes=[pltpu.VMEM((tt, d), F32)],
        compiler_params=_cparams(2),
        name="peer_dense",
    )(h2, w3, u_bf, v_bf, x, mod)


def _peer(h2, x, mod, w_q, subkeys, u_tab, v_tab):
    b, s, d = x.shape
    h2f = h2.reshape(b * s, d)
    wq_hi, wq_lo = _split_bf16(w_q)
    sk_hi, sk_lo = _split_bf16(subkeys)
    w3 = _peer_route(h2f, wq_hi, wq_lo, sk_hi, sk_lo)
    out = _peer_dense(h2f, w3, u_tab.astype(BF16), v_tab.astype(BF16),
                      x.reshape(b * s, d), mod, s)
    return out.reshape(b, s, d)


def kernel(x, c, positions, tok_norm, ch_norm, ada_w, ada_b, conv_in_w, conv_w, conv_out_w,
           mla_in_w, q_a_norm, kv_a_norm, q_b_w, kv_b_w, q_norm, k_norm, mla_out_w,
           peer_q_w, peer_subkeys, peer_u, peer_v):
    b, s, d = x.shape
    depth = ada_w.shape[0]
    bp = -(-b // SUBLANES) * SUBLANES
    c_pad = jnp.zeros((bp, d), F32).at[:b].set(c)
    mod_all = _ada(c_pad, ada_w, ada_b)[:, :b].reshape(depth, b, 6, d)

    row = lambda a: a.reshape(1, -1)
    for i in range(depth):
        mod = mod_all[i]
        j = i // 2
        if i % 2 == 0:
            x, h2 = _conv_layer(x, mod, row(tok_norm[i]), row(ch_norm[i]),
                                conv_in_w[j].astype(BF16), conv_w[j],
                                conv_out_w[j].astype(BF16))
        else:
            w_in = mla_in_w[j].astype(BF16)
            wq = w_in[:, :Q_LORA]
            wkv = w_in[:, Q_LORA:Q_LORA + KV_LORA]
            wkr = jnp.pad(w_in[:, Q_LORA + KV_LORA:], ((0, 0), (0, LANES - QK_ROPE)))
            wqb = q_b_w[j].astype(BF16).reshape(Q_LORA, MLA_HEADS, QK_HEAD)
            wqbn = wqb[:, :, :QK_NOPE].reshape(Q_LORA, MLA_HEADS * QK_NOPE)
            wqbr = wqb[:, :, QK_NOPE:].reshape(Q_LORA, MLA_HEADS * QK_ROPE)
            inv_freq = ROPE_THETA ** (-jnp.arange(0, QK_ROPE, 2, dtype=F32) / QK_ROPE)
            freq = jnp.concatenate([inv_freq, inv_freq]).reshape(1, QK_ROPE)
            half = QK_ROPE // 2
            sgn = jnp.concatenate([-jnp.ones((half,), F32), jnp.ones((half,), F32)]).reshape(1, QK_ROPE)
            q, k, v = _mla_proj(
                x, mod, row(tok_norm[i]), positions.reshape(b, s, 1), freq, sgn,
                wq, wkv, wkr, row(q_a_norm[j]), row(kv_a_norm[j]),
                wqbn, wqbr, kv_b_w[j].astype(BF16),
                row(q_norm[j][:QK_NOPE]), row(q_norm[j][QK_NOPE:]),
                row(k_norm[j][:QK_NOPE]), row(k_norm[j][QK_NOPE:]))
            o = _flash(q, k, v)
            x, h2 = _mla_out(o, x, mod, row(ch_norm[i]), mla_out_w[j].astype(BF16))
        x = _peer(h2, x, mod, peer_q_w[i], peer_subkeys[i], peer_u[i], peer_v[i])
    return x
```

```python
import functools
import math

import jax
import jax.numpy as jnp
from jax import lax
from jax.experimental import pallas as pl
from jax.experimental.pallas import tpu as pltpu

F32 = jnp.float32
BF16 = jnp.bfloat16

D_MODEL = 1024
MLA_HEADS = 8
Q_LORA = 384
KV_LORA = 256
QK_NOPE = 128
QK_ROPE = 64
QK_HEAD = QK_NOPE + QK_ROPE
QK_PAD = 256
V_HEAD = 128
ROPE_THETA = 10000.0
PEER_HEADS = 8
PEER_KEYS = 128
PEER_EXPERTS = PEER_KEYS * PEER_KEYS
PEER_HALF = 128
PEER_TOPK = 16
NORM_EPS = 1e-6

LANES = 128
SUBLANES = 8
VMEM_LIMIT = 56 * 1024 * 1024

NEG_INF = float("-inf")
_NT = (((1,), (1,)), ((), ()))


def _cparams(n_axes):
    return pltpu.CompilerParams(
        dimension_semantics=("arbitrary",) * n_axes, vmem_limit_bytes=VMEM_LIMIT)


def _split_bf16(a):
    hi = a.astype(BF16)
    lo = (a - hi.astype(F32)).astype(BF16)
    return hi, lo


def _dot3(a_hi, a_lo, b_hi, b_lo, dims=None):
    if dims is None:
        f = lambda a, b: jnp.dot(a, b, preferred_element_type=F32)
    else:
        f = lambda a, b: lax.dot_general(a, b, dims, preferred_element_type=F32)
    return f(a_hi, b_hi) + (f(a_hi, b_lo) + f(a_lo, b_hi))


def _modulate(x, g, shift, scale):
    ms = jnp.mean(x * x, axis=-1, keepdims=True)
    y = x * lax.rsqrt(ms + NORM_EPS)
    return (y * g) * (1.0 + scale) + shift


def _ada_kernel(c_ref, w_ref, b_ref, o_ref):
    c = c_ref[...]
    ca = c * jax.nn.sigmoid(c)
    o_ref[0] = jnp.dot(ca, w_ref[0], preferred_element_type=F32,
                       precision=lax.Precision.HIGHEST) + b_ref[0]


def _ada(c_pad, ada_w, ada_b):
    depth, d, d6 = ada_w.shape
    bp = c_pad.shape[0]
    nb = d6 // d
    return pl.pallas_call(
        _ada_kernel,
        grid=(depth, nb),
        in_specs=[
            pl.BlockSpec((bp, d), lambda i, j: (0, 0)),
            pl.BlockSpec((1, d, d), lambda i, j: (i, 0, j)),
            pl.BlockSpec((1, 1, d), lambda i, j: (i, 0, j)),
        ],
        out_specs=pl.BlockSpec((1, bp, d), lambda i, j: (i, 0, j)),
        out_shape=jax.ShapeDtypeStruct((depth, bp, d6), F32),
        compiler_params=_cparams(2),
        name="ada",
    )(c_pad, ada_w, ada_b.reshape(depth, 1, d6))


def _conv_layer_kernel(x_ref, mod_ref, tn_ref, cn_ref, win_ref, cw_ref, wout_ref,
                       x1_ref, h2_ref, ubuf, *, ts):
    d = D_MODEL
    x = x_ref[0]
    m = mod_ref[0]
    sh1, sc1, g1, sh2, sc2 = m[0:1], m[1:2], m[2:3], m[3:4], m[4:5]
    h = _modulate(x, tn_ref[...], sh1, sc1)
    bcv = jnp.dot(h.astype(BF16), win_ref[...], preferred_element_type=F32)
    b_gate = bcv[:, :d]
    u = bcv[:, d:2 * d] * bcv[:, 2 * d:]

    @pl.when(pl.program_id(1) == 0)
    def _():
        ubuf[0:SUBLANES, :] = jnp.zeros((SUBLANES, d), F32)

    ubuf[SUBLANES:SUBLANES + ts, :] = u
    u1 = ubuf[SUBLANES - 1:SUBLANES - 1 + ts, :]
    u2 = ubuf[SUBLANES - 2:SUBLANES - 2 + ts, :]
    cw = cw_ref[...]
    conv = cw[0:1] * u2 + cw[1:2] * u1 + cw[2:3] * u
    ubuf[0:SUBLANES, :] = u[ts - SUBLANES:, :]
    y = jnp.dot((b_gate * conv).astype(BF16), wout_ref[...], preferred_element_type=F32)
    x1 = x + g1 * y
    x1_ref[0] = x1
    h2_ref[0] = _modulate(x1, cn_ref[...], sh2, sc2)


def _conv_layer(x, mod, tok_norm, ch_norm, w_in, conv_w, w_out):
    b, s, d = x.shape
    ts = min(512, s)
    kern = functools.partial(_conv_layer_kernel, ts=ts)
    const = lambda *shape: pl.BlockSpec(shape, lambda i, j: (0,) * len(shape))
    return pl.pallas_call(
        kern,
        grid=(b, s // ts),
        in_specs=[
            pl.BlockSpec((1, ts, d), lambda i, j: (i, j, 0)),
            pl.BlockSpec((1, 6, d), lambda i, j: (i, 0, 0)),
            const(1, d), const(1, d), const(d, 3 * d), const(3, d), const(d, d),
        ],
        out_specs=[pl.BlockSpec((1, ts, d), lambda i, j: (i, j, 0))] * 2,
        out_shape=[jax.ShapeDtypeStruct((b, s, d), F32)] * 2,
        scratch_shapes=[pltpu.VMEM((ts + SUBLANES, d), F32)],
        compiler_params=_cparams(2),
        name="conv_layer",
    )(x, mod, tok_norm, ch_norm, w_in, conv_w, w_out)


def _mla_proj_kernel(x_ref, mod_ref, tn_ref, pos_ref, freq_ref, sgn_ref,
                     wq_ref, wkv_ref, wkr_ref, qag_ref, kvag_ref,
                     wqbn_ref, wqbr_ref, wkvb_ref, qgn_ref, qgr_ref, kgn_ref, kgr_ref,
                     q_ref, k_ref, v_ref, *, ts):
    x = x_ref[0]
    m = mod_ref[0]
    h = _modulate(x, tn_ref[...], m[0:1], m[1:2]).astype(BF16)
    q_lat = jnp.dot(h, wq_ref[...], preferred_element_type=F32)
    kv_lat = jnp.dot(h, wkv_ref[...], preferred_element_type=F32)
    k_rope = jnp.dot(h, wkr_ref[...], preferred_element_type=F32)[:, :QK_ROPE]

    def rms(v, g):
        return v * lax.rsqrt(jnp.mean(v * v, axis=-1, keepdims=True) + NORM_EPS) * g

    qn = rms(q_lat, qag_ref[...]).astype(BF16)
    kvn = rms(kv_lat, kvag_ref[...]).astype(BF16)
    q_nope = jnp.dot(qn, wqbn_ref[...], preferred_element_type=F32)
    q_rope = jnp.dot(qn, wqbr_ref[...], preferred_element_type=F32)
    kv = jnp.dot(kvn, wkvb_ref[...], preferred_element_type=F32)

    ang = pos_ref[0].astype(F32) * freq_ref[...]
    cos = jnp.cos(ang)
    sin_signed = jnp.sin(ang) * sgn_ref[...]

    def rope(v):
        half = QK_ROPE // 2
        rot = jnp.concatenate([v[:, half:], v[:, :half]], axis=1)
        return v * cos + rot * sin_signed

    scale = QK_HEAD ** -0.5
    kr_g = rope(k_rope * kgr_ref[...])
    kr_ss = jnp.sum(k_rope * k_rope, axis=-1, keepdims=True)
    zpad = jnp.zeros((ts, QK_PAD - QK_HEAD), F32)
    for hd in range(MLA_HEADS):
        qn_h = q_nope[:, hd * QK_NOPE:(hd + 1) * QK_NOPE]
        qr_h = q_rope[:, hd * QK_ROPE:(hd + 1) * QK_ROPE]
        ss = jnp.sum(qn_h * qn_h, axis=-1, keepdims=True) + jnp.sum(qr_h * qr_h, axis=-1, keepdims=True)
        rinv = lax.rsqrt(ss * (1.0 / QK_HEAD) + NORM_EPS) * scale
        qa = qn_h * rinv * qgn_ref[...]
        qb = rope(qr_h * qgr_ref[...]) * rinv
        q_ref[0, hd] = jnp.concatenate([qa, qb, zpad], axis=1).astype(BF16)

        kn_h = kv[:, hd * 2 * QK_NOPE:hd * 2 * QK_NOPE + QK_NOPE]
        v_h = kv[:, hd * 2 * QK_NOPE + QK_NOPE:(hd + 1) * 2 * QK_NOPE]
        kss = jnp.sum(kn_h * kn_h, axis=-1, keepdims=True) + kr_ss
        krinv = lax.rsqrt(kss * (1.0 / QK_HEAD) + NORM_EPS)
        ka = kn_h * krinv * kgn_ref[...]
        kb = kr_g * krinv
        k_ref[0, hd] = jnp.concatenate([ka, kb, zpad], axis=1).astype(BF16)
        v_ref[0, hd] = v_h.astype(BF16)


def _mla_proj(x, mod, tok_norm, pos3, freq, sgn, wq, wkv, wkr, qag, kvag,
              wqbn, wqbr, wkvb, qgn, qgr, kgn, kgr):
    b, s, d = x.shape
    ts = min(512, s)
    kern = functools.partial(_mla_proj_kernel, ts=ts)
    consts = [tok_norm]
    rest = [freq, sgn, wq, wkv, wkr, qag, kvag, wqbn, wqbr, wkvb, qgn, qgr, kgn, kgr]
    cspec = lambda a: pl.BlockSpec(a.shape, lambda i, j: (0,) * a.ndim)
    hq = pl.BlockSpec((1, MLA_HEADS, ts, QK_PAD), lambda i, j: (i, 0, j, 0))
    hv = pl.BlockSpec((1, MLA_HEADS, ts, V_HEAD), lambda i, j: (i, 0, j, 0))
    return pl.pallas_call(
        kern,
        grid=(b, s // ts),
        in_specs=[
            pl.BlockSpec((1, ts, d), lambda i, j: (i, j, 0)),
            pl.BlockSpec((1, 6, d), lambda i, j: (i, 0, 0)),
            cspec(tok_norm),
            pl.BlockSpec((1, ts, 1), lambda i, j: (i, j, 0)),
        ] + [cspec(a) for a in rest],
        out_specs=[hq, hq, hv],
        out_shape=[
            jax.ShapeDtypeStruct((b, MLA_HEADS, s, QK_PAD), BF16),
            jax.ShapeDtypeStruct((b, MLA_HEADS, s, QK_PAD), BF16),
            jax.ShapeDtypeStruct((b, MLA_HEADS, s, V_HEAD), BF16),
        ],
        compiler_params=_cparams(2),
        name="mla_proj",
    )(x, mod, tok_norm, pos3, *rest)


FLASH_HEADS = 4


def _flash_kernel(q_ref, k_ref, v_ref, o_ref, *, tq):
    i = pl.program_id(2)

    def block(j, carries, diagonal):
        start = pl.multiple_of(j * tq, tq)
        out = []
        for hh in range(FLASH_HEADS):
            m, l, acc = carries[hh]
            kj = k_ref[0, hh, pl.ds(start, tq), :]
            vj = v_ref[0, hh, pl.ds(start, tq), :]
            s = lax.dot_general(q_ref[0, hh], kj, _NT, preferred_element_type=F32)
            if diagonal:
                row = lax.broadcasted_iota(jnp.int32, (tq, tq), 0)
                col = lax.broadcasted_iota(jnp.int32, (tq, tq), 1)
                s = jnp.where(row >= col, s, NEG_INF)
            m_new = jnp.maximum(m, jnp.max(s, axis=-1, keepdims=True))
            p = jnp.exp(s - m_new)
            alpha = jnp.exp(m - m_new)
            l = alpha * l + jnp.sum(p, axis=-1, keepdims=True)
            acc = alpha * acc + jnp.dot(p.astype(BF16), vj, preferred_element_type=F32)
            out.append((m_new, l, acc))
        return tuple(out)

    init = tuple((jnp.full((tq, 1), NEG_INF, F32), jnp.zeros((tq, 1), F32),
                  jnp.zeros((tq, V_HEAD), F32)) for _ in range(FLASH_HEADS))
    carries = lax.fori_loop(0, i, lambda j, c: block(j, c, False), init)
    carries = block(i, carries, True)
    o_ref[0] = jnp.concatenate([(acc / l).astype(BF16) for _, l, acc in carries], axis=1)


def _flash(q, k, v):
    b, hds, s, _ = q.shape
    tq = min(512, s)
    kern = functools.partial(_flash_kernel, tq=tq)
    fh = FLASH_HEADS
    return pl.pallas_call(
        kern,
        grid=(b, hds // fh, s // tq),
        in_specs=[
            pl.BlockSpec((1, fh, tq, QK_PAD), lambda bi, h, i: (bi, h, i, 0)),
            pl.BlockSpec((1, fh, s, QK_PAD), lambda bi, h, i: (bi, h, 0, 0)),
            pl.BlockSpec((1, fh, s, V_HEAD), lambda bi, h, i: (bi, h, 0, 0)),
        ],
        out_specs=pl.BlockSpec((1, tq, fh * V_HEAD), lambda bi, h, i: (bi, i, h)),
        out_shape=jax.ShapeDtypeStruct((b, s, hds * V_HEAD), BF16),
        compiler_params=_cparams(3),
        name="mla_flash",
    )(q, k, v)


def _mla_out_kernel(o_ref, x_ref, mod_ref, cn_ref, w_ref, x1_ref, h2_ref):
    m = mod_ref[0]
    y = jnp.dot(o_ref[0], w_ref[...], preferred_element_type=F32)
    x1 = x_ref[0] + m[2:3] * y
    x1_ref[0] = x1
    h2_ref[0] = _modulate(x1, cn_ref[...], m[3:4], m[4:5])


def _mla_out(o, x, mod, ch_norm, w_out):
    b, s, d = x.shape
    ts = min(512, s)
    return pl.pallas_call(
        _mla_out_kernel,
        grid=(b, s // ts),
        in_specs=[
            pl.BlockSpec((1, ts, d), lambda i, j: (i, j, 0)),
            pl.BlockSpec((1, ts, d), lambda i, j: (i, j, 0)),
            pl.BlockSpec((1, 6, d), lambda i, j: (i, 0, 0)),
            pl.BlockSpec((1, d), lambda i, j: (0, 0)),
            pl.BlockSpec((d, d), lambda i, j: (0, 0)),
        ],
        out_specs=[pl.BlockSpec((1, ts, d), lambda i, j: (i, j, 0))] * 2,
        out_shape=[jax.ShapeDtypeStruct((b, s, d), F32)] * 2,
        compiler_params=_cparams(2),
        name="mla_out",
    )(o, x, mod, ch_norm, w_out)


def _acc_row(acc, iota, r, row):
    return jnp.where(iota == r, row, acc)


def _sort16_network():
    def merge(lo, hi, r):
        step = r * 2
        if step < hi - lo:
            yield from merge(lo, hi, step)
            yield from merge(lo + r, hi, step)
            yield from ((i, i + r) for i in range(lo + r, hi - r, step))
        else:
            yield (lo, lo + r)

    def sort(lo, hi):
        if hi - lo >= 1:
            mid = lo + (hi - lo) // 2
            yield from sort(lo, mid)
            yield from sort(mid + 1, hi)
            yield from merge(lo, hi, 1)

    return tuple(sort(0, PEER_TOPK - 1))


_SORT16 = _sort16_network()


def _top16_keys(s, key, rank):
    n, l = s.shape
    depth = n // SUBLANES
    val = [s[SUBLANES * v:SUBLANES * (v + 1), :] for v in range(depth)]
    pay = [key[SUBLANES * v:SUBLANES * (v + 1), :] for v in range(depth)]
    for a, b in _SORT16:
        swap = (val[b] > val[a]) | ((val[b] == val[a]) & (pay[b] < pay[a]))
        val[a], val[b] = jnp.where(swap, val[b], val[a]), jnp.where(swap, val[a], val[b])
        pay[a], pay[b] = jnp.where(swap, pay[b], pay[a]), jnp.where(swap, pay[a], pay[b])
    vals = jnp.zeros((PEER_TOPK, l), F32)
    idxs = jnp.zeros((PEER_TOPK, l), F32)
    for r in range(PEER_TOPK):
        mx = jnp.max(val[0], axis=0, keepdims=True)
        idx = jnp.min(jnp.where(val[0] == mx, pay[0], float(n)), axis=0, keepdims=True)
        hit = pay[0] == idx
        vals = _acc_row(vals, rank, float(r), mx)
        idxs = _acc_row(idxs, rank, float(r), idx)
        for p in range(PEER_TOPK - 1 - r):
            val[p] = jnp.where(hit, val[p + 1], val[p])
            pay[p] = jnp.where(hit, pay[p + 1], pay[p])
    return vals, idxs


def _top16_pairs(s0, i0, s1, i1, rank):
    l = s0.shape[1]
    sub = lax.broadcasted_iota(jnp.int32, (SUBLANES, l), 0)
    subf = sub.astype(F32)
    lists = []
    for p in range(PEER_TOPK):
        plane = s0[0:SUBLANES] + s1[p:p + 1]
        n_valid = PEER_TOPK // (p + 1)
        if n_valid < SUBLANES:
            plane = jnp.where(sub < n_valid, plane, NEG_INF)
        lists.append(plane)
    tail = s0[SUBLANES:] + s1[0:1]
    pos_tail = (subf + float(SUBLANES)) * float(PEER_TOPK)
    pops = jnp.zeros((SUBLANES, l), F32)
    no_pos = float(PEER_TOPK * PEER_TOPK)
    vals = jnp.zeros((PEER_TOPK, l), F32)
    poss = jnp.zeros((PEER_TOPK, l), F32)
    for r in range(PEER_TOPK):
        pos_head = subf * float(PEER_TOPK) + pops
        mx = jnp.max(jnp.maximum(lists[0], tail), axis=0, keepdims=True)
        cand = jnp.minimum(jnp.where(lists[0] == mx, pos_head, no_pos),
                           jnp.where(tail == mx, pos_tail, no_pos))
        pos = jnp.min(cand, axis=0, keepdims=True)
        hit = pos_head == pos
        vals = _acc_row(vals, rank, float(r), mx)
        poss = _acc_row(poss, rank, float(r), pos)
        pops = pops + jnp.where(hit, 1.0, 0.0)
        for p in range(PEER_TOPK - 1 - r):
            lists[p] = jnp.where(hit, lists[p + 1], lists[p])
        tail = jnp.where(pos_tail == pos, NEG_INF, tail)
    pi = jnp.floor(poss * (1.0 / PEER_TOPK))
    pj = poss - pi * float(PEER_TOPK)
    ia = jnp.zeros((PEER_TOPK, l), F32)
    ib = jnp.zeros((PEER_TOPK, l), F32)
    for k in range(PEER_TOPK):
        ia = jnp.where(pi == float(k), i0[k:k + 1], ia)
        ib = jnp.where(pj == float(k), i1[k:k + 1], ib)
    return vals, ia, ib


W_PITCH = 136


def _peer_route_kernel(h_ref, wqh_ref, wql_ref, skh_ref, skl_ref, w_ref, hb_ref,
                       hlo_s, iat_s, ibt_s, gt_s, ia_s, ib_s, ghi_s, glo_s, wbuf, *, tt):

    @pl.when(pl.program_id(0) == 0)
    def _():
        for ref in (ia_s, ib_s, ghi_s, glo_s):
            ref[...] = jnp.zeros_like(ref)

    h_hi, h_lo = _split_bf16(h_ref[...])
    hb_ref[...] = h_hi
    hlo_s[...] = h_lo

    def one_head(hd, key, rank):
        cols = slice(hd * 2 * PEER_HALF, (hd + 1) * 2 * PEER_HALF)
        q = _dot3(hb_ref[...], hlo_s[...], wqh_ref[:, cols], wql_ref[:, cols])
        tops = []
        for p in range(2):
            q_hi, q_lo = _split_bf16(q[:, p * PEER_HALF:(p + 1) * PEER_HALF])
            s = _dot3(skh_ref[hd, p], skl_ref[hd, p], q_hi, q_lo, _NT)
            tops.append(_top16_keys(s, key, rank))
        (s0, i0), (s1, i1) = tops
        best, ia, ib = _top16_pairs(s0, i0, s1, i1, rank)
        e = jnp.exp(best - best[0:1])
        g = e / jnp.sum(e, axis=0, keepdims=True)
        rows = pl.ds(pl.multiple_of(hd * PEER_TOPK, PEER_TOPK), PEER_TOPK)
        iat_s[rows, :] = ia
        ibt_s[rows, :] = ib
        gt_s[rows, :] = g

    onehot_key = lax.broadcasted_iota(jnp.int32, (PEER_KEYS, LANES), 0).astype(F32).astype(BF16)
    zero = jnp.zeros((PEER_KEYS, LANES), BF16)
    one = jnp.ones((PEER_KEYS, LANES), BF16)
    pack_rows = 2 * SUBLANES

    def row_plane(ref, t):
        tile = jnp.broadcast_to(ref[t:t + 1, :], (pack_rows, LANES)).astype(BF16)
        return pltpu.repeat(tile, PEER_KEYS // pack_rows, axis=0)

    def scatter_token(t):
        hit_a = row_plane(ia_s, t) == onehot_key
        hit_b = row_plane(ib_s, t) == onehot_key
        a_hi = jnp.where(hit_a, row_plane(ghi_s, t), zero)
        a_lo = jnp.where(hit_a, row_plane(glo_s, t), zero)
        b_one = jnp.where(hit_b, one, zero)
        lhs = jnp.concatenate([a_hi, a_lo], axis=1)
        rhs = jnp.concatenate([b_one, b_one], axis=1)
        w = lax.dot_general(lhs, rhs, _NT, preferred_element_type=F32)
        wbuf[pl.ds(t, PEER_KEYS, stride=W_PITCH), :] = w

    key = lax.broadcasted_iota(jnp.int32, (PEER_KEYS, tt), 0).astype(F32)
    rank = lax.broadcasted_iota(jnp.int32, (PEER_TOPK, tt), 0).astype(F32)
    per_head = tt // PEER_HEADS
    for hd in range(PEER_HEADS):
        one_head(hd, key, rank)
        for t in range(hd * per_head, (hd + 1) * per_head):
            scatter_token(t)
    for ia in range(PEER_KEYS):
        w_ref[ia] = wbuf[ia * W_PITCH:ia * W_PITCH + tt, :]

    ia_s[...] = iat_s[...].T
    ib_s[...] = ibt_s[...].T
    g = gt_s[...].T
    g_hi = g.astype(BF16).astype(F32)
    ghi_s[...] = g_hi
    glo_s[...] = g - g_hi


def _peer_route(h2, wq_hi, wq_lo, sk_hi, sk_lo):
    t, d = h2.shape
    tt = 128
    n = t // tt
    kern = functools.partial(_peer_route_kernel, tt=tt)
    cspec = lambda a: pl.BlockSpec(a.shape, lambda i: (0,) * a.ndim)
    slab = pltpu.VMEM((PEER_HEADS * PEER_TOPK, tt), F32)
    tok = pltpu.VMEM((tt, PEER_HEADS * PEER_TOPK), F32)
    cur = lambda i: (jnp.minimum(i, n - 1), 0)
    return pl.pallas_call(
        kern,
        grid=(n + 1,),
        in_specs=[pl.BlockSpec((tt, d), cur),
                  cspec(wq_hi), cspec(wq_lo), cspec(sk_hi), cspec(sk_lo)],
        out_specs=[pl.BlockSpec((PEER_KEYS, tt, PEER_KEYS),
                                lambda i: (0, jnp.maximum(i - 1, 0), 0)),
                   pl.BlockSpec((tt, d), cur)],
        out_shape=[jax.ShapeDtypeStruct((PEER_KEYS, t, PEER_KEYS), F32),
                   jax.ShapeDtypeStruct((t, d), BF16)],
        scratch_shapes=[pltpu.VMEM((tt, d), BF16), slab, slab, slab, tok, tok, tok, tok,
                        pltpu.VMEM((PEER_KEYS * W_PITCH, PEER_KEYS), F32)],
        compiler_params=_cparams(1),
        name="peer_route",
    )(h2, wq_hi, wq_lo, sk_hi, sk_lo)


def _gelu(z):
    return 0.5 * z * (1.0 + lax.erf(z * (1.0 / math.sqrt(2.0))))


def _peer_dense_kernel(h_ref, w_ref, u_ref, v_ref, x_ref, mod_ref, o_ref, acc_ref, *, n_ia):
    j = pl.program_id(1)

    @pl.when(j == 0)
    def _():
        acc_ref[...] = jnp.zeros_like(acc_ref)

    z = lax.dot_general(h_ref[...], u_ref[...], _NT, preferred_element_type=F32)
    w = jnp.concatenate([w_ref[i] for i in range(n_ia)], axis=1)
    a = (_gelu(z) * w).astype(BF16)
    acc_ref[...] += jnp.dot(a, v_ref[...], preferred_element_type=F32)

    @pl.when(j == pl.num_programs(1) - 1)
    def _():
        o_ref[...] = x_ref[...] + mod_ref[0][5:6] * acc_ref[...]


def _peer_dense(h_bf, w3, u_bf, v_bf, x, mod, seq):
    t, d = h_bf.shape
    tt = min(1024, seq)
    n_ia = SUBLANES
    e_blk = n_ia * PEER_KEYS
    kern = functools.partial(_peer_dense_kernel, n_ia=n_ia)
    return pl.pallas_call(
        kern,
        grid=(t // tt, PEER_EXPERTS // e_blk),
        in_specs=[
            pl.BlockSpec((tt, d), lambda i, j: (i, 0)),
            pl.BlockSpec((n_ia, tt, PEER_KEYS), lambda i, j: (j, i, 0)),
            pl.BlockSpec((e_blk, d), lambda i, j: (j, 0)),
            pl.BlockSpec((e_blk, d), lambda i, j: (j, 0)),
            pl.BlockSpec((tt, d), lambda i, j: (i, 0)),
            pl.BlockSpec((1, 6, d), lambda i, j: ((i * tt) // seq, 0, 0)),
        ],
        out_specs=pl.BlockSpec((tt, d), lambda i, j: (i, 0)),
        out_shape=jax.ShapeDtypeStruct((t, d), F32),
        scratch_shapes=[pltpu.VMEM((tt, d), F32)],
        compiler_params=_cparams(2),
        name="peer_dense",
    )(h_bf, w3, u_bf, v_bf, x, mod)


def _peer(h2, x, mod, w_q, subkeys, u_tab, v_tab):
    b, s, d = x.shape
    h2f = h2.reshape(b * s, d)
    wq_hi, wq_lo = _split_bf16(w_q)
    sk_hi, sk_lo = _split_bf16(subkeys)
    w3, h_bf = _peer_route(h2f, wq_hi, wq_lo, sk_hi, sk_lo)
    out = _peer_dense(h_bf, w3, u_tab.astype(BF16), v_tab.astype(BF16),
                      x.reshape(b * s, d), mod, s)
    return out.reshape(b, s, d)


def kernel(x, c, positions, tok_norm, ch_norm, ada_w, ada_b, conv_in_w, conv_w, conv_out_w,
           mla_in_w, q_a_norm, kv_a_norm, q_b_w, kv_b_w, q_norm, k_norm, mla_out_w,
           peer_q_w, peer_subkeys, peer_u, peer_v):
    b, s, d = x.shape
    depth = ada_w.shape[0]
    bp = -(-b // SUBLANES) * SUBLANES
    c_pad = jnp.zeros((bp, d), F32).at[:b].set(c)
    mod_all = _ada(c_pad, ada_w, ada_b)[:, :b].reshape(depth, b, 6, d)

    row = lambda a: a.reshape(1, -1)
    for i in range(depth):
        mod = mod_all[i]
        j = i // 2
        if i % 2 == 0:
            x, h2 = _conv_layer(x, mod, row(tok_norm[i]), row(ch_norm[i]),
                                conv_in_w[j].astype(BF16), conv_w[j],
                                conv_out_w[j].astype(BF16))
        else:
            w_in = mla_in_w[j].astype(BF16)
            wq = w_in[:, :Q_LORA]
            wkv = w_in[:, Q_LORA:Q_LORA + KV_LORA]
            wkr = jnp.pad(w_in[:, Q_LORA + KV_LORA:], ((0, 0), (0, LANES - QK_ROPE)))
            wqb = q_b_w[j].astype(BF16).reshape(Q_LORA, MLA_HEADS, QK_HEAD)
            wqbn = wqb[:, :, :QK_NOPE].reshape(Q_LORA, MLA_HEADS * QK_NOPE)
            wqbr = wqb[:, :, QK_NOPE:].reshape(Q_LORA, MLA_HEADS * QK_ROPE)
            inv_freq = ROPE_THETA ** (-jnp.arange(0, QK_ROPE, 2, dtype=F32) / QK_ROPE)
            freq = jnp.concatenate([inv_freq, inv_freq]).reshape(1, QK_ROPE)
            half = QK_ROPE // 2
            sgn = jnp.concatenate([-jnp.ones((half,), F32), jnp.ones((half,), F32)]).reshape(1, QK_ROPE)
            q, k, v = _mla_proj(
                x, mod, row(tok_norm[i]), positions.reshape(b, s, 1), freq, sgn,
                wq, wkv, wkr, row(q_a_norm[j]), row(kv_a_norm[j]),
                wqbn, wqbr, kv_b_w[j].astype(BF16),
                row(q_norm[j][:QK_NOPE]), row(q_norm[j][QK_NOPE:]),
                row(k_norm[j][:QK_NOPE]), row(k_norm[j][QK_NOPE:]))
            o = _flash(q, k, v)
            x, h2 = _mla_out(o, x, mod, row(ch_norm[i]), mla_out_w[j].astype(BF16))
        x = _peer(h2, x, mod, peer_q_w[i], peer_subkeys[i], peer_u[i], peer_v[i])
    return x
```

```python
import functools
import math

import jax
import jax.numpy as jnp
from jax import lax
from jax.experimental import pallas as pl
from jax.experimental.pallas import tpu as pltpu

F32 = jnp.float32
BF16 = jnp.bfloat16

D_MODEL = 1024
MLA_HEADS = 8
Q_LORA = 384
KV_LORA = 256
QK_NOPE = 128
QK_ROPE = 64
QK_HEAD = QK_NOPE + QK_ROPE
QK_PAD = 256
V_HEAD = 128
ROPE_THETA = 10000.0
PEER_HEADS = 8
PEER_KEYS = 128
PEER_EXPERTS = PEER_KEYS * PEER_KEYS
PEER_HALF = 128
PEER_TOPK = 16
NORM_EPS = 1e-6

LANES = 128
SUBLANES = 8
VMEM_LIMIT = 56 * 1024 * 1024

NEG_INF = float("-inf")
_NT = (((1,), (1,)), ((), ()))


def _cparams(n_axes):
    return pltpu.CompilerParams(
        dimension_semantics=("arbitrary",) * n_axes, vmem_limit_bytes=VMEM_LIMIT)


def _split_bf16(a):
    hi = a.astype(BF16)
    lo = (a - hi.astype(F32)).astype(BF16)
    return hi, lo


def _dot3(a_hi, a_lo, b_hi, b_lo, dims=None):
    if dims is None:
        f = lambda a, b: jnp.dot(a, b, preferred_element_type=F32)
    else:
        f = lambda a, b: lax.dot_general(a, b, dims, preferred_element_type=F32)
    return f(a_hi, b_hi) + (f(a_hi, b_lo) + f(a_lo, b_hi))


def _modulate(x, g, shift, scale):
    ms = jnp.mean(x * x, axis=-1, keepdims=True)
    y = x * lax.rsqrt(ms + NORM_EPS)
    return (y * g) * (1.0 + scale) + shift


def _ada_kernel(c_ref, w_ref, b_ref, o_ref):
    c = c_ref[...]
    ca = c * jax.nn.sigmoid(c)
    o_ref[0] = jnp.dot(ca, w_ref[0], preferred_element_type=F32,
                       precision=lax.Precision.HIGHEST) + b_ref[0]


def _ada(c_pad, ada_w, ada_b):
    depth, d, d6 = ada_w.shape
    bp = c_pad.shape[0]
    nb = d6 // d
    return pl.pallas_call(
        _ada_kernel,
        grid=(depth, nb),
        in_specs=[
            pl.BlockSpec((bp, d), lambda i, j: (0, 0)),
            pl.BlockSpec((1, d, d), lambda i, j: (i, 0, j)),
            pl.BlockSpec((1, 1, d), lambda i, j: (i, 0, j)),
        ],
        out_specs=pl.BlockSpec((1, bp, d), lambda i, j: (i, 0, j)),
        out_shape=jax.ShapeDtypeStruct((depth, bp, d6), F32),
        compiler_params=_cparams(2),
        name="ada",
    )(c_pad, ada_w, ada_b.reshape(depth, 1, d6))


def _conv_layer_kernel(x_ref, mod_ref, tn_ref, cn_ref, win_ref, cw_ref, wout_ref,
                       x1_ref, h2_ref, ubuf, *, ts):
    d = D_MODEL
    x = x_ref[0]
    m = mod_ref[0]
    sh1, sc1, g1, sh2, sc2 = m[0:1], m[1:2], m[2:3], m[3:4], m[4:5]
    h = _modulate(x, tn_ref[...], sh1, sc1)
    bcv = jnp.dot(h.astype(BF16), win_ref[...], preferred_element_type=F32)
    b_gate = bcv[:, :d]
    u = bcv[:, d:2 * d] * bcv[:, 2 * d:]

    @pl.when(pl.program_id(1) == 0)
    def _():
        ubuf[0:SUBLANES, :] = jnp.zeros((SUBLANES, d), F32)

    ubuf[SUBLANES:SUBLANES + ts, :] = u
    u1 = ubuf[SUBLANES - 1:SUBLANES - 1 + ts, :]
    u2 = ubuf[SUBLANES - 2:SUBLANES - 2 + ts, :]
    cw = cw_ref[...]
    conv = cw[0:1] * u2 + cw[1:2] * u1 + cw[2:3] * u
    ubuf[0:SUBLANES, :] = u[ts - SUBLANES:, :]
    y = jnp.dot((b_gate * conv).astype(BF16), wout_ref[...], preferred_element_type=F32)
    x1 = x + g1 * y
    x1_ref[0] = x1
    h2_ref[0] = _modulate(x1, cn_ref[...], sh2, sc2)


def _conv_layer(x, mod, tok_norm, ch_norm, w_in, conv_w, w_out):
    b, s, d = x.shape
    ts = min(512, s)
    kern = functools.partial(_conv_layer_kernel, ts=ts)
    const = lambda *shape: pl.BlockSpec(shape, lambda i, j: (0,) * len(shape))
    return pl.pallas_call(
        kern,
        grid=(b, s // ts),
        in_specs=[
            pl.BlockSpec((1, ts, d), lambda i, j: (i, j, 0)),
            pl.BlockSpec((1, 6, d), lambda i, j: (i, 0, 0)),
            const(1, d), const(1, d), const(d, 3 * d), const(3, d), const(d, d),
        ],
        out_specs=[pl.BlockSpec((1, ts, d), lambda i, j: (i, j, 0))] * 2,
        out_shape=[jax.ShapeDtypeStruct((b, s, d), F32)] * 2,
        scratch_shapes=[pltpu.VMEM((ts + SUBLANES, d), F32)],
        compiler_params=_cparams(2),
        name="conv_layer",
    )(x, mod, tok_norm, ch_norm, w_in, conv_w, w_out)


def _mla_proj_kernel(x_ref, mod_ref, tn_ref, pos_ref, freq_ref, sgn_ref,
                     wq_ref, wkv_ref, wkr_ref, qag_ref, kvag_ref,
                     wqbn_ref, wqbr_ref, wkvb_ref, qgn_ref, qgr_ref, kgn_ref, kgr_ref,
                     q_ref, k_ref, v_ref, *, ts):
    x = x_ref[0]
    m = mod_ref[0]
    h = _modulate(x, tn_ref[...], m[0:1], m[1:2]).astype(BF16)
    q_lat = jnp.dot(h, wq_ref[...], preferred_element_type=F32)
    kv_lat = jnp.dot(h, wkv_ref[...], preferred_element_type=F32)
    k_rope = jnp.dot(h, wkr_ref[...], preferred_element_type=F32)[:, :QK_ROPE]

    def rms(v, g):
        return v * lax.rsqrt(jnp.mean(v * v, axis=-1, keepdims=True) + NORM_EPS) * g

    qn = rms(q_lat, qag_ref[...]).astype(BF16)
    kvn = rms(kv_lat, kvag_ref[...]).astype(BF16)
    q_nope = jnp.dot(qn, wqbn_ref[...], preferred_element_type=F32)
    q_rope = jnp.dot(qn, wqbr_ref[...], preferred_element_type=F32)
    kv = jnp.dot(kvn, wkvb_ref[...], preferred_element_type=F32)

    ang = pos_ref[0].astype(F32) * freq_ref[...]
    cos = jnp.cos(ang)
    sin_signed = jnp.sin(ang) * sgn_ref[...]

    def rope(v):
        half = QK_ROPE // 2
        rot = jnp.concatenate([v[:, half:], v[:, :half]], axis=1)
        return v * cos + rot * sin_signed

    scale = QK_HEAD ** -0.5 * math.log2(math.e)
    kr_g = rope(k_rope * kgr_ref[...])
    kr_ss = jnp.sum(k_rope * k_rope, axis=-1, keepdims=True)
    zpad = jnp.zeros((ts, QK_PAD - QK_HEAD), F32)
    for hd in range(MLA_HEADS):
        qn_h = q_nope[:, hd * QK_NOPE:(hd + 1) * QK_NOPE]
        qr_h = q_rope[:, hd * QK_ROPE:(hd + 1) * QK_ROPE]
        ss = jnp.sum(qn_h * qn_h, axis=-1, keepdims=True) + jnp.sum(qr_h * qr_h, axis=-1, keepdims=True)
        rinv = lax.rsqrt(ss * (1.0 / QK_HEAD) + NORM_EPS) * scale
        qa = qn_h * rinv * qgn_ref[...]
        qb = rope(qr_h * qgr_ref[...]) * rinv
        q_ref[0, hd] = jnp.concatenate([qa, qb, zpad], axis=1).astype(BF16)

        kn_h = kv[:, hd * 2 * QK_NOPE:hd * 2 * QK_NOPE + QK_NOPE]
        v_h = kv[:, hd * 2 * QK_NOPE + QK_NOPE:(hd + 1) * 2 * QK_NOPE]
        kss = jnp.sum(kn_h * kn_h, axis=-1, keepdims=True) + kr_ss
        krinv = lax.rsqrt(kss * (1.0 / QK_HEAD) + NORM_EPS)
        ka = kn_h * krinv * kgn_ref[...]
        kb = kr_g * krinv
        k_ref[0, hd] = jnp.concatenate([ka, kb, zpad], axis=1).astype(BF16)
        v_ref[0, hd] = v_h.astype(BF16)


def _mla_proj(x, mod, tok_norm, pos3, freq, sgn, wq, wkv, wkr, qag, kvag,
              wqbn, wqbr, wkvb, qgn, qgr, kgn, kgr):
    b, s, d = x.shape
    ts = min(512, s)
    kern = functools.partial(_mla_proj_kernel, ts=ts)
    consts = [tok_norm]
    rest = [freq, sgn, wq, wkv, wkr, qag, kvag, wqbn, wqbr, wkvb, qgn, qgr, kgn, kgr]
    cspec = lambda a: pl.BlockSpec(a.shape, lambda i, j: (0,) * a.ndim)
    hq = pl.BlockSpec((1, MLA_HEADS, ts, QK_PAD), lambda i, j: (i, 0, j, 0))
    hv = pl.BlockSpec((1, MLA_HEADS, ts, V_HEAD), lambda i, j: (i, 0, j, 0))
    return pl.pallas_call(
        kern,
        grid=(b, s // ts),
        in_specs=[
            pl.BlockSpec((1, ts, d), lambda i, j: (i, j, 0)),
            pl.BlockSpec((1, 6, d), lambda i, j: (i, 0, 0)),
            cspec(tok_norm),
            pl.BlockSpec((1, ts, 1), lambda i, j: (i, j, 0)),
        ] + [cspec(a) for a in rest],
        out_specs=[hq, hq, hv],
        out_shape=[
            jax.ShapeDtypeStruct((b, MLA_HEADS, s, QK_PAD), BF16),
            jax.ShapeDtypeStruct((b, MLA_HEADS, s, QK_PAD), BF16),
            jax.ShapeDtypeStruct((b, MLA_HEADS, s, V_HEAD), BF16),
        ],
        compiler_params=_cparams(2),
        name="mla_proj",
    )(x, mod, tok_norm, pos3, *rest)


FLASH_HEADS = 4


def _flash_kernel(q_ref, k_ref, v_ref, o_ref, *, tq):
    i = pl.program_id(2)

    def block(j, carries, diagonal):
        start = pl.multiple_of(j * tq, tq)
        out = []
        for hh in range(FLASH_HEADS):
            m, l, acc = carries[hh]
            kj = k_ref[0, hh, pl.ds(start, tq), :]
            vj = v_ref[0, hh, pl.ds(start, tq), :]
            s = lax.dot_general(q_ref[0, hh], kj, _NT, preferred_element_type=F32)
            if diagonal:
                row = lax.broadcasted_iota(jnp.int32, (tq, tq), 0)
                col = lax.broadcasted_iota(jnp.int32, (tq, tq), 1)
                s = jnp.where(row >= col, s, NEG_INF)
            m_new = jnp.maximum(m, jnp.max(s, axis=-1, keepdims=True))
            p = jnp.exp2(s - m_new)
            alpha = jnp.exp2(m - m_new)
            l = alpha * l + jnp.sum(p, axis=-1, keepdims=True)
            acc = alpha * acc + jnp.dot(p.astype(BF16), vj, preferred_element_type=F32)
            out.append((m_new, l, acc))
        return tuple(out)

    init = tuple((jnp.full((tq, 1), NEG_INF, F32), jnp.zeros((tq, 1), F32),
                  jnp.zeros((tq, V_HEAD), F32)) for _ in range(FLASH_HEADS))
    carries = lax.fori_loop(0, i, lambda j, c: block(j, c, False), init)
    carries = block(i, carries, True)
    o_ref[0] = jnp.concatenate([(acc / l).astype(BF16) for _, l, acc in carries], axis=1)


def _flash(q, k, v):
    b, hds, s, _ = q.shape
    tq = min(512, s)
    kern = functools.partial(_flash_kernel, tq=tq)
    fh = FLASH_HEADS
    return pl.pallas_call(
        kern,
        grid=(b, hds // fh, s // tq),
        in_specs=[
            pl.BlockSpec((1, fh, tq, QK_PAD), lambda bi, h, i: (bi, h, i, 0)),
            pl.BlockSpec((1, fh, s, QK_PAD), lambda bi, h, i: (bi, h, 0, 0)),
            pl.BlockSpec((1, fh, s, V_HEAD), lambda bi, h, i: (bi, h, 0, 0)),
        ],
        out_specs=pl.BlockSpec((1, tq, fh * V_HEAD), lambda bi, h, i: (bi, i, h)),
        out_shape=jax.ShapeDtypeStruct((b, s, hds * V_HEAD), BF16),
        compiler_params=_cparams(3),
        name="mla_flash",
    )(q, k, v)


def _mla_out_kernel(o_ref, x_ref, mod_ref, cn_ref, w_ref, x1_ref, h2_ref):
    m = mod_ref[0]
    y = jnp.dot(o_ref[0], w_ref[...], preferred_element_type=F32)
    x1 = x_ref[0] + m[2:3] * y
    x1_ref[0] = x1
    h2_ref[0] = _modulate(x1, cn_ref[...], m[3:4], m[4:5])


def _mla_out(o, x, mod, ch_norm, w_out):
    b, s, d = x.shape
    ts = min(512, s)
    return pl.pallas_call(
        _mla_out_kernel,
        grid=(b, s // ts),
        in_specs=[
            pl.BlockSpec((1, ts, d), lambda i, j: (i, j, 0)),
            pl.BlockSpec((1, ts, d), lambda i, j: (i, j, 0)),
            pl.BlockSpec((1, 6, d), lambda i, j: (i, 0, 0)),
            pl.BlockSpec((1, d), lambda i, j: (0, 0)),
            pl.BlockSpec((d, d), lambda i, j: (0, 0)),
        ],
        out_specs=[pl.BlockSpec((1, ts, d), lambda i, j: (i, j, 0))] * 2,
        out_shape=[jax.ShapeDtypeStruct((b, s, d), F32)] * 2,
        compiler_params=_cparams(2),
        name="mla_out",
    )(o, x, mod, ch_norm, w_out)


def _acc_row(acc, iota, r, row):
    return jnp.where(iota == r, row, acc)


def _sort16_network():
    def merge(lo, hi, r):
        step = r * 2
        if step < hi - lo:
            yield from merge(lo, hi, step)
            yield from merge(lo + r, hi, step)
            yield from ((i, i + r) for i in range(lo + r, hi - r, step))
        else:
            yield (lo, lo + r)

    def sort(lo, hi):
        if hi - lo >= 1:
            mid = lo + (hi - lo) // 2
            yield from sort(lo, mid)
            yield from sort(mid + 1, hi)
            yield from merge(lo, hi, 1)

    return tuple(sort(0, PEER_TOPK - 1))


_SORT16 = _sort16_network()


def _top16_keys(s, key, rank):
    n, l = s.shape
    depth = n // SUBLANES
    val = [s[SUBLANES * v:SUBLANES * (v + 1), :] for v in range(depth)]
    pay = [key[SUBLANES * v:SUBLANES * (v + 1), :] for v in range(depth)]
    for a, b in _SORT16:
        swap = (val[b] > val[a]) | ((val[b] == val[a]) & (pay[b] < pay[a]))
        val[a], val[b] = jnp.where(swap, val[b], val[a]), jnp.where(swap, val[a], val[b])
        pay[a], pay[b] = jnp.where(swap, pay[b], pay[a]), jnp.where(swap, pay[a], pay[b])
    vals = jnp.zeros((PEER_TOPK, l), F32)
    idxs = jnp.zeros((PEER_TOPK, l), F32)
    for r in range(PEER_TOPK):
        mx = jnp.max(val[0], axis=0, keepdims=True)
        idx = jnp.min(jnp.where(val[0] == mx, pay[0], float(n)), axis=0, keepdims=True)
        hit = pay[0] == idx
        vals = _acc_row(vals, rank, float(r), mx)
        idxs = _acc_row(idxs, rank, float(r), idx)
        for p in range(PEER_TOPK - 1 - r):
            val[p] = jnp.where(hit, val[p + 1], val[p])
            pay[p] = jnp.where(hit, pay[p + 1], pay[p])
    return vals, idxs


def _top16_pairs(s0, i0, s1, i1, rank):
    l = s0.shape[1]
    sub = lax.broadcasted_iota(jnp.int32, (SUBLANES, l), 0)
    subf = sub.astype(F32)
    lists = []
    for p in range(PEER_TOPK):
        plane = s0[0:SUBLANES] + s1[p:p + 1]
        n_valid = PEER_TOPK // (p + 1)
        if n_valid < SUBLANES:
            plane = jnp.where(sub < n_valid, plane, NEG_INF)
        lists.append(plane)
    tail = s0[SUBLANES:] + s1[0:1]
    pos_tail = (subf + float(SUBLANES)) * float(PEER_TOPK)
    pops = jnp.zeros((SUBLANES, l), F32)
    no_pos = float(PEER_TOPK * PEER_TOPK)
    vals = jnp.zeros((PEER_TOPK, l), F32)
    poss = jnp.zeros((PEER_TOPK, l), F32)
    for r in range(PEER_TOPK):
        pos_head = subf * float(PEER_TOPK) + pops
        mx = jnp.max(jnp.maximum(lists[0], tail), axis=0, keepdims=True)
        cand = jnp.minimum(jnp.where(lists[0] == mx, pos_head, no_pos),
                           jnp.where(tail == mx, pos_tail, no_pos))
        pos = jnp.min(cand, axis=0, keepdims=True)
        hit = pos_head == pos
        vals = _acc_row(vals, rank, float(r), mx)
        poss = _acc_row(poss, rank, float(r), pos)
        pops = pops + jnp.where(hit, 1.0, 0.0)
        for p in range(PEER_TOPK - 1 - r):
            lists[p] = jnp.where(hit, lists[p + 1], lists[p])
        tail = jnp.where(pos_tail == pos, NEG_INF, tail)
    pi = jnp.floor(poss * (1.0 / PEER_TOPK))
    pj = poss - pi * float(PEER_TOPK)
    ia = jnp.zeros((PEER_TOPK, l), F32)
    ib = jnp.zeros((PEER_TOPK, l), F32)
    for k in range(PEER_TOPK):
        ia = jnp.where(pi == float(k), i0[k:k + 1], ia)
        ib = jnp.where(pj == float(k), i1[k:k + 1], ib)
    return vals, ia, ib


W_PITCH = 136


def _peer_route_kernel(h_ref, wq_ref, sk_ref, w_ref, hb_ref,
                       iat_s, ibt_s, gt_s, ia_s, ib_s, ghi_s, glo_s, wbuf, *, tt):

    @pl.when(pl.program_id(0) == 0)
    def _():
        for ref in (ia_s, ib_s, ghi_s, glo_s):
            ref[...] = jnp.zeros_like(ref)

    hb_ref[...] = h_ref[...].astype(BF16)

    def one_head(hd, key, rank):
        cols = slice(hd * 2 * PEER_HALF, (hd + 1) * 2 * PEER_HALF)
        q = jnp.dot(hb_ref[...], wq_ref[:, cols], preferred_element_type=F32).astype(BF16)
        tops = []
        for p in range(2):
            s = lax.dot_general(sk_ref[hd, p], q[:, p * PEER_HALF:(p + 1) * PEER_HALF], _NT,
                                preferred_element_type=F32)
            tops.append(_top16_keys(s, key, rank))
        (s0, i0), (s1, i1) = tops
        best, ia, ib = _top16_pairs(s0, i0, s1, i1, rank)
        e = jnp.exp(best - best[0:1])
        g = e / jnp.sum(e, axis=0, keepdims=True)
        rows = pl.ds(pl.multiple_of(hd * PEER_TOPK, PEER_TOPK), PEER_TOPK)
        iat_s[rows, :] = ia
        ibt_s[rows, :] = ib
        gt_s[rows, :] = g

    onehot_key = lax.broadcasted_iota(jnp.int32, (PEER_KEYS, LANES), 0).astype(F32).astype(BF16)
    zero = jnp.zeros((PEER_KEYS, LANES), BF16)
    one = jnp.ones((PEER_KEYS, LANES), BF16)
    pack_rows = 2 * SUBLANES

    def row_plane(ref, t):
        tile = jnp.broadcast_to(ref[t:t + 1, :], (pack_rows, LANES)).astype(BF16)
        return jnp.concatenate([tile] * (PEER_KEYS // pack_rows), axis=0)

    def scatter_token(t):
        hit_a = row_plane(ia_s, t) == onehot_key
        hit_b = row_plane(ib_s, t) == onehot_key
        a_hi = jnp.where(hit_a, row_plane(ghi_s, t), zero)
        a_lo = jnp.where(hit_a, row_plane(glo_s, t), zero)
        b_one = jnp.where(hit_b, one, zero)
        lhs = jnp.concatenate([a_hi, a_lo], axis=1)
        rhs = jnp.concatenate([b_one, b_one], axis=1)
        w = lax.dot_general(lhs, rhs, _NT, preferred_element_type=F32)
        wbuf[pl.ds(t, PEER_KEYS, stride=W_PITCH), :] = w

    key = lax.broadcasted_iota(jnp.int32, (PEER_KEYS, tt), 0).astype(F32)
    rank = lax.broadcasted_iota(jnp.int32, (PEER_TOPK, tt), 0).astype(F32)
    per_head = tt // PEER_HEADS
    for hd in range(PEER_HEADS):
        one_head(hd, key, rank)
        for t in range(hd * per_head, (hd + 1) * per_head):
            scatter_token(t)
    for ia in range(PEER_KEYS):
        w_ref[ia] = wbuf[ia * W_PITCH:ia * W_PITCH + tt, :]

    ia_s[...] = iat_s[...].T
    ib_s[...] = ibt_s[...].T
    g = gt_s[...].T
    g_hi = g.astype(BF16).astype(F32)
    ghi_s[...] = g_hi
    glo_s[...] = g - g_hi


def _peer_route(h2, wq, sk):
    t, d = h2.shape
    tt = 128
    n = t // tt
    kern = functools.partial(_peer_route_kernel, tt=tt)
    cspec = lambda a: pl.BlockSpec(a.shape, lambda i: (0,) * a.ndim)
    slab = pltpu.VMEM((PEER_HEADS * PEER_TOPK, tt), F32)
    tok = pltpu.VMEM((tt, PEER_HEADS * PEER_TOPK), F32)
    cur = lambda i: (jnp.minimum(i, n - 1), 0)
    return pl.pallas_call(
        kern,
        grid=(n + 1,),
        in_specs=[pl.BlockSpec((tt, d), cur),
                  cspec(wq), cspec(sk)],
        out_specs=[pl.BlockSpec((PEER_KEYS, tt, PEER_KEYS),
                                lambda i: (0, jnp.maximum(i - 1, 0), 0)),
                   pl.BlockSpec((tt, d), cur)],
        out_shape=[jax.ShapeDtypeStruct((PEER_KEYS, t, PEER_KEYS), F32),
                   jax.ShapeDtypeStruct((t, d), BF16)],
        scratch_shapes=[slab, slab, slab, tok, tok, tok, tok,
                        pltpu.VMEM((PEER_KEYS * W_PITCH, PEER_KEYS), F32)],
        compiler_params=_cparams(1),
        name="peer_route",
    )(h2, wq, sk)


def _gelu(z):
    return 0.5 * z * (1.0 + lax.erf(z * (1.0 / math.sqrt(2.0))))


def _peer_dense_kernel(h_ref, w_ref, u_ref, v_ref, x_ref, mod_ref, o_ref, acc_ref, *, n_ia):
    j = pl.program_id(1)

    @pl.when(j == 0)
    def _():
        acc_ref[...] = jnp.zeros_like(acc_ref)

    z = lax.dot_general(h_ref[...], u_ref[...], _NT, preferred_element_type=F32)
    w = jnp.concatenate([w_ref[i] for i in range(n_ia)], axis=1)
    a = (_gelu(z) * w).astype(BF16)
    acc_ref[...] += jnp.dot(a, v_ref[...], preferred_element_type=F32)

    @pl.when(j == pl.num_programs(1) - 1)
    def _():
        o_ref[...] = x_ref[...] + mod_ref[0][5:6] * acc_ref[...]


def _peer_dense(h_bf, w3, u_all, v_all, layer, x, mod, seq):
    t, d = h_bf.shape
    tt = min(1024, seq)
    n_ia = SUBLANES
    e_blk = n_ia * PEER_KEYS
    kern = functools.partial(_peer_dense_kernel, n_ia=n_ia)
    return pl.pallas_call(
        kern,
        grid=(t // tt, PEER_EXPERTS // e_blk),
        in_specs=[
            pl.BlockSpec((tt, d), lambda i, j: (i, 0)),
            pl.BlockSpec((n_ia, tt, PEER_KEYS), lambda i, j: (j, i, 0)),
            pl.BlockSpec((None, e_blk, d), lambda i, j: (layer, j, 0)),
            pl.BlockSpec((None, e_blk, d), lambda i, j: (layer, j, 0)),
            pl.BlockSpec((tt, d), lambda i, j: (i, 0)),
            pl.BlockSpec((1, 6, d), lambda i, j: ((i * tt) // seq, 0, 0)),
        ],
        out_specs=pl.BlockSpec((tt, d), lambda i, j: (i, 0)),
        out_shape=jax.ShapeDtypeStruct((t, d), F32),
        scratch_shapes=[pltpu.VMEM((tt, d), F32)],
        compiler_params=_cparams(2),
        name="peer_dense",
    )(h_bf, w3, u_all, v_all, x, mod)


def _peer(h2, x, mod, w_q, subkeys, u_all, v_all, layer):
    b, s, d = x.shape
    h2f = h2.reshape(b * s, d)
    w3, h_bf = _peer_route(h2f, w_q.astype(BF16), subkeys.astype(BF16))
    out = _peer_dense(h_bf, w3, u_all, v_all, layer, x.reshape(b * s, d), mod, s)
    return out.reshape(b, s, d)


def kernel(x, c, positions, tok_norm, ch_norm, ada_w, ada_b, conv_in_w, conv_w, conv_out_w,
           mla_in_w, q_a_norm, kv_a_norm, q_b_w, kv_b_w, q_norm, k_norm, mla_out_w,
           peer_q_w, peer_subkeys, peer_u, peer_v):
    b, s, d = x.shape
    depth = ada_w.shape[0]
    bp = -(-b // SUBLANES) * SUBLANES
    c_pad = jnp.zeros((bp, d), F32).at[:b].set(c)
    mod_all = _ada(c_pad, ada_w, ada_b)[:, :b].reshape(depth, b, 6, d)

    u_all = peer_u.astype(BF16)
    v_all = peer_v.astype(BF16)
    row = lambda a: a.reshape(1, -1)
    for i in range(depth):
        mod = mod_all[i]
        j = i // 2
        if i % 2 == 0:
            x, h2 = _conv_layer(x, mod, row(tok_norm[i]), row(ch_norm[i]),
                                conv_in_w[j].astype(BF16), conv_w[j],
                                conv_out_w[j].astype(BF16))
        else:
            w_in = mla_in_w[j].astype(BF16)
            wq = w_in[:, :Q_LORA]
            wkv = w_in[:, Q_LORA:Q_LORA + KV_LORA]
            wkr = jnp.pad(w_in[:, Q_LORA + KV_LORA:], ((0, 0), (0, LANES - QK_ROPE)))
            wqb = q_b_w[j].astype(BF16).reshape(Q_LORA, MLA_HEADS, QK_HEAD)
            wqbn = wqb[:, :, :QK_NOPE].reshape(Q_LORA, MLA_HEADS * QK_NOPE)
            wqbr = wqb[:, :, QK_NOPE:].reshape(Q_LORA, MLA_HEADS * QK_ROPE)
            inv_freq = ROPE_THETA ** (-jnp.arange(0, QK_ROPE, 2, dtype=F32) / QK_ROPE)
            freq = jnp.concatenate([inv_freq, inv_freq]).reshape(1, QK_ROPE)
            half = QK_ROPE // 2
            sgn = jnp.concatenate([-jnp.ones((half,), F32), jnp.ones((half,), F32)]).reshape(1, QK_ROPE)
            q, k, v = _mla_proj(
                x, mod, row(tok_norm[i]), positions.reshape(b, s, 1), freq, sgn,
                wq, wkv, wkr, row(q_a_norm[j]), row(kv_a_norm[j]),
                wqbn, wqbr, kv_b_w[j].astype(BF16),
                row(q_norm[j][:QK_NOPE]), row(q_norm[j][QK_NOPE:]),
                row(k_norm[j][:QK_NOPE]), row(k_norm[j][QK_NOPE:]))
            o = _flash(q, k, v)
            x, h2 = _mla_out(o, x, mod, row(ch_norm[i]), mla_out_w[j].astype(BF16))
        x = _peer(h2, x, mod, peer_q_w[i], peer_subkeys[i], u_all, v_all, i)
    return x
```

```python
import functools
import math

import jax
import jax.numpy as jnp
from jax import lax
from jax.experimental import pallas as pl
from jax.experimental.pallas import tpu as pltpu

F32 = jnp.float32
BF16 = jnp.bfloat16

D_MODEL = 1024
MLA_HEADS = 8
Q_LORA = 384
KV_LORA = 256
QK_NOPE = 128
QK_ROPE = 64
QK_HEAD = QK_NOPE + QK_ROPE
QK_PAD = 256
V_HEAD = 128
ROPE_THETA = 10000.0
PEER_HEADS = 8
PEER_KEYS = 128
PEER_EXPERTS = PEER_KEYS * PEER_KEYS
PEER_HALF = 128
PEER_TOPK = 16
NORM_EPS = 1e-6

LANES = 128
SUBLANES = 8
VMEM_LIMIT = 56 * 1024 * 1024

NEG_INF = float("-inf")
_NT = (((1,), (1,)), ((), ()))


def _cparams(n_axes):
    return pltpu.CompilerParams(
        dimension_semantics=("arbitrary",) * n_axes, vmem_limit_bytes=VMEM_LIMIT)


def _split_bf16(a):
    hi = a.astype(BF16)
    lo = (a - hi.astype(F32)).astype(BF16)
    return hi, lo


def _dot3(a_hi, a_lo, b_hi, b_lo, dims=None):
    if dims is None:
        f = lambda a, b: jnp.dot(a, b, preferred_element_type=F32)
    else:
        f = lambda a, b: lax.dot_general(a, b, dims, preferred_element_type=F32)
    return f(a_hi, b_hi) + (f(a_hi, b_lo) + f(a_lo, b_hi))


def _modulate(x, g, shift, scale):
    ms = jnp.mean(x * x, axis=-1, keepdims=True)
    y = x * lax.rsqrt(ms + NORM_EPS)
    return (y * g) * (1.0 + scale) + shift


def _ada_kernel(c_ref, w_ref, b_ref, o_ref):
    c = c_ref[...]
    ca = c * jax.nn.sigmoid(c)
    o_ref[0] = jnp.dot(ca, w_ref[0], preferred_element_type=F32,
                       precision=lax.Precision.HIGHEST) + b_ref[0]


def _ada(c_pad, ada_w, ada_b):
    depth, d, d6 = ada_w.shape
    bp = c_pad.shape[0]
    nb = d6 // d
    return pl.pallas_call(
        _ada_kernel,
        grid=(depth, nb),
        in_specs=[
            pl.BlockSpec((bp, d), lambda i, j: (0, 0)),
            pl.BlockSpec((1, d, d), lambda i, j: (i, 0, j)),
            pl.BlockSpec((1, 1, d), lambda i, j: (i, 0, j)),
        ],
        out_specs=pl.BlockSpec((1, bp, d), lambda i, j: (i, 0, j)),
        out_shape=jax.ShapeDtypeStruct((depth, bp, d6), F32),
        compiler_params=_cparams(2),
        name="ada",
    )(c_pad, ada_w, ada_b.reshape(depth, 1, d6))


def _conv_layer_kernel(x_ref, mod_ref, tn_ref, cn_ref, win_ref, cw_ref, wout_ref,
                       x1_ref, h2_ref, ubuf, *, ts):
    d = D_MODEL
    x = x_ref[0]
    m = mod_ref[0]
    sh1, sc1, g1, sh2, sc2 = m[0:1], m[1:2], m[2:3], m[3:4], m[4:5]
    h = _modulate(x, tn_ref[...], sh1, sc1)
    bcv = jnp.dot(h.astype(BF16), win_ref[...], preferred_element_type=F32)
    b_gate = bcv[:, :d]
    u = bcv[:, d:2 * d] * bcv[:, 2 * d:]

    @pl.when(pl.program_id(1) == 0)
    def _():
        ubuf[0:SUBLANES, :] = jnp.zeros((SUBLANES, d), F32)

    ubuf[SUBLANES:SUBLANES + ts, :] = u
    u1 = ubuf[SUBLANES - 1:SUBLANES - 1 + ts, :]
    u2 = ubuf[SUBLANES - 2:SUBLANES - 2 + ts, :]
    cw = cw_ref[...]
    conv = cw[0:1] * u2 + cw[1:2] * u1 + cw[2:3] * u
    ubuf[0:SUBLANES, :] = u[ts - SUBLANES:, :]
    y = jnp.dot((b_gate * conv).astype(BF16), wout_ref[...], preferred_element_type=F32)
    x1 = x + g1 * y
    x1_ref[0] = x1
    h2_ref[0] = _modulate(x1, cn_ref[...], sh2, sc2)


def _conv_layer(x, mod, tok_norm, ch_norm, w_in, conv_w, w_out):
    b, s, d = x.shape
    ts = min(512, s)
    kern = functools.partial(_conv_layer_kernel, ts=ts)
    const = lambda *shape: pl.BlockSpec(shape, lambda i, j: (0,) * len(shape))
    return pl.pallas_call(
        kern,
        grid=(b, s // ts),
        in_specs=[
            pl.BlockSpec((1, ts, d), lambda i, j: (i, j, 0)),
            pl.BlockSpec((1, 6, d), lambda i, j: (i, 0, 0)),
            const(1, d), const(1, d), const(d, 3 * d), const(3, d), const(d, d),
        ],
        out_specs=[pl.BlockSpec((1, ts, d), lambda i, j: (i, j, 0))] * 2,
        out_shape=[jax.ShapeDtypeStruct((b, s, d), F32)] * 2,
        scratch_shapes=[pltpu.VMEM((ts + SUBLANES, d), F32)],
        compiler_params=_cparams(2),
        name="conv_layer",
    )(x, mod, tok_norm, ch_norm, w_in, conv_w, w_out)


def _mla_proj_kernel(x_ref, mod_ref, tn_ref, pos_ref, freq_ref, sgn_ref,
                     wq_ref, wkv_ref, wkr_ref, qag_ref, kvag_ref,
                     wqbn_ref, wqbr_ref, wkvb_ref, qgn_ref, qgr_ref, kgn_ref, kgr_ref,
                     q_ref, k_ref, v_ref, *, ts):
    x = x_ref[0]
    m = mod_ref[0]
    h = _modulate(x, tn_ref[...], m[0:1], m[1:2]).astype(BF16)
    q_lat = jnp.dot(h, wq_ref[...], preferred_element_type=F32)
    kv_lat = jnp.dot(h, wkv_ref[...], preferred_element_type=F32)
    k_rope = jnp.dot(h, wkr_ref[...], preferred_element_type=F32)

    def rms(v, g):
        return v * lax.rsqrt(jnp.mean(v * v, axis=-1, keepdims=True) + NORM_EPS) * g

    qn = rms(q_lat, qag_ref[...]).astype(BF16)
    kvn = rms(kv_lat, kvag_ref[...]).astype(BF16)
    q_nope = jnp.dot(qn, wqbn_ref[...], preferred_element_type=F32)
    q_rope = jnp.dot(qn, wqbr_ref[...], preferred_element_type=F32)
    kv = jnp.dot(kvn, wkvb_ref[...], preferred_element_type=F32)

    ang = pos_ref[0].astype(F32) * freq_ref[...]
    cos = jnp.cos(ang)
    sin_signed = jnp.sin(ang) * sgn_ref[...]
    lane = lax.broadcasted_iota(jnp.int32, (ts, LANES), 1)
    first_half = (lane % QK_ROPE) < QK_ROPE // 2
    block_of = [lane < QK_ROPE, lane >= QK_ROPE]

    def rope(v):
        rot = jnp.where(first_half, pltpu.roll(v, LANES - QK_ROPE // 2, axis=1),
                        pltpu.roll(v, QK_ROPE // 2, axis=1))
        return v * cos + rot * sin_signed

    scale = QK_HEAD ** -0.5 * math.log2(math.e)
    kr_g = rope(k_rope * kgr_ref[...])
    kr_ss = 0.5 * jnp.sum(k_rope * k_rope, axis=-1, keepdims=True)
    pair_raw, pair_g = None, None
    for hd in range(MLA_HEADS):
        own = block_of[hd % 2]
        if hd % 2 == 0:
            pair_raw = q_rope[:, (hd // 2) * LANES:(hd // 2 + 1) * LANES]
            pair_g = rope(pair_raw * qgr_ref[...])
        qn_h = q_nope[:, hd * QK_NOPE:(hd + 1) * QK_NOPE]
        qr_h = jnp.where(own, pair_raw, 0.0)
        ss = jnp.sum(qn_h * qn_h + qr_h * qr_h, axis=-1, keepdims=True)
        rinv = lax.rsqrt(ss * (1.0 / QK_HEAD) + NORM_EPS) * scale
        qa = qn_h * rinv * qgn_ref[...]
        qb = jnp.where(own, pair_g, 0.0) * rinv
        q_ref[0, hd] = jnp.concatenate([qa, qb], axis=1).astype(BF16)

        kn_h = kv[:, hd * 2 * QK_NOPE:hd * 2 * QK_NOPE + QK_NOPE]
        v_h = kv[:, hd * 2 * QK_NOPE + QK_NOPE:(hd + 1) * 2 * QK_NOPE]
        kss = jnp.sum(kn_h * kn_h, axis=-1, keepdims=True) + kr_ss
        krinv = lax.rsqrt(kss * (1.0 / QK_HEAD) + NORM_EPS)
        ka = kn_h * krinv * kgn_ref[...]
        kb = jnp.where(own, kr_g, 0.0) * krinv
        k_ref[0, hd] = jnp.concatenate([ka, kb], axis=1).astype(BF16)
        v_ref[0, hd] = v_h.astype(BF16)


def _mla_proj(x, mod, tok_norm, pos3, freq, sgn, wq, wkv, wkr, qag, kvag,
              wqbn, wqbr, wkvb, qgn, qgr, kgn, kgr):
    b, s, d = x.shape
    ts = min(512, s)
    kern = functools.partial(_mla_proj_kernel, ts=ts)
    consts = [tok_norm]
    rest = [freq, sgn, wq, wkv, wkr, qag, kvag, wqbn, wqbr, wkvb, qgn, qgr, kgn, kgr]
    cspec = lambda a: pl.BlockSpec(a.shape, lambda i, j: (0,) * a.ndim)
    hq = pl.BlockSpec((1, MLA_HEADS, ts, QK_PAD), lambda i, j: (i, 0, j, 0))
    hv = pl.BlockSpec((1, MLA_HEADS, ts, V_HEAD), lambda i, j: (i, 0, j, 0))
    return pl.pallas_call(
        kern,
        grid=(b, s // ts),
        in_specs=[
            pl.BlockSpec((1, ts, d), lambda i, j: (i, j, 0)),
            pl.BlockSpec((1, 6, d), lambda i, j: (i, 0, 0)),
            cspec(tok_norm),
            pl.BlockSpec((1, ts, 1), lambda i, j: (i, j, 0)),
        ] + [cspec(a) for a in rest],
        out_specs=[hq, hq, hv],
        out_shape=[
            jax.ShapeDtypeStruct((b, MLA_HEADS, s, QK_PAD), BF16),
            jax.ShapeDtypeStruct((b, MLA_HEADS, s, QK_PAD), BF16),
            jax.ShapeDtypeStruct((b, MLA_HEADS, s, V_HEAD), BF16),
        ],
        compiler_params=_cparams(2),
        name="mla_proj",
    )(x, mod, tok_norm, pos3, *rest)


FLASH_HEADS = 4


def _flash_kernel(q_ref, k_ref, v_ref, o_ref, *, tq):
    i = pl.program_id(2)

    def block(j, carries, diagonal):
        start = pl.multiple_of(j * tq, tq)
        out = []
        for hh in range(FLASH_HEADS):
            m, l, acc = carries[hh]
            kj = k_ref[0, hh, pl.ds(start, tq), :]
            vj = v_ref[0, hh, pl.ds(start, tq), :]
            s = lax.dot_general(q_ref[0, hh], kj, _NT, preferred_element_type=F32)
            if diagonal:
                row = lax.broadcasted_iota(jnp.int32, (tq, tq), 0)
                col = lax.broadcasted_iota(jnp.int32, (tq, tq), 1)
                s = jnp.where(row >= col, s, NEG_INF)
            m_new = jnp.maximum(m, jnp.max(s, axis=-1, keepdims=True))
            p = jnp.exp2(s - m_new)
            alpha = jnp.exp2(m - m_new)
            l = alpha * l + jnp.sum(p, axis=-1, keepdims=True)
            acc = alpha * acc + jnp.dot(p.astype(BF16), vj, preferred_element_type=F32)
            out.append((m_new, l, acc))
        return tuple(out)

    init = tuple((jnp.full((tq, 1), NEG_INF, F32), jnp.zeros((tq, 1), F32),
                  jnp.zeros((tq, V_HEAD), F32)) for _ in range(FLASH_HEADS))
    carries = lax.fori_loop(0, i, lambda j, c: block(j, c, False), init)
    carries = block(i, carries, True)
    o_ref[0] = jnp.concatenate([(acc / l).astype(BF16) for _, l, acc in carries], axis=1)


def _flash(q, k, v):
    b, hds, s, _ = q.shape
    tq = min(512, s)
    kern = functools.partial(_flash_kernel, tq=tq)
    fh = FLASH_HEADS
    return pl.pallas_call(
        kern,
        grid=(b, hds // fh, s // tq),
        in_specs=[
            pl.BlockSpec((1, fh, tq, QK_PAD), lambda bi, h, i: (bi, h, i, 0)),
            pl.BlockSpec((1, fh, s, QK_PAD), lambda bi, h, i: (bi, h, 0, 0)),
            pl.BlockSpec((1, fh, s, V_HEAD), lambda bi, h, i: (bi, h, 0, 0)),
        ],
        out_specs=pl.BlockSpec((1, tq, fh * V_HEAD), lambda bi, h, i: (bi, i, h)),
        out_shape=jax.ShapeDtypeStruct((b, s, hds * V_HEAD), BF16),
        compiler_params=_cparams(3),
        name="mla_flash",
    )(q, k, v)


def _mla_out_kernel(o_ref, x_ref, mod_ref, cn_ref, w_ref, x1_ref, h2_ref):
    m = mod_ref[0]
    y = jnp.dot(o_ref[0], w_ref[...], preferred_element_type=F32)
    x1 = x_ref[0] + m[2:3] * y
    x1_ref[0] = x1
    h2_ref[0] = _modulate(x1, cn_ref[...], m[3:4], m[4:5])


def _mla_out(o, x, mod, ch_norm, w_out):
    b, s, d = x.shape
    ts = min(512, s)
    return pl.pallas_call(
        _mla_out_kernel,
        grid=(b, s // ts),
        in_specs=[
            pl.BlockSpec((1, ts, d), lambda i, j: (i, j, 0)),
            pl.BlockSpec((1, ts, d), lambda i, j: (i, j, 0)),
            pl.BlockSpec((1, 6, d), lambda i, j: (i, 0, 0)),
            pl.BlockSpec((1, d), lambda i, j: (0, 0)),
            pl.BlockSpec((d, d), lambda i, j: (0, 0)),
        ],
        out_specs=[pl.BlockSpec((1, ts, d), lambda i, j: (i, j, 0))] * 2,
        out_shape=[jax.ShapeDtypeStruct((b, s, d), F32)] * 2,
        compiler_params=_cparams(2),
        name="mla_out",
    )(o, x, mod, ch_norm, w_out)


def _acc_row(acc, iota, r, row):
    return jnp.where(iota == r, row, acc)


def _sort16_network():
    def merge(lo, hi, r):
        step = r * 2
        if step < hi - lo:
            yield from merge(lo, hi, step)
            yield from merge(lo + r, hi, step)
            yield from ((i, i + r) for i in range(lo + r, hi - r, step))
        else:
            yield (lo, lo + r)

    def sort(lo, hi):
        if hi - lo >= 1:
            mid = lo + (hi - lo) // 2
            yield from sort(lo, mid)
            yield from sort(mid + 1, hi)
            yield from merge(lo, hi, 1)

    return tuple(sort(0, PEER_TOPK - 1))


_SORT16 = _sort16_network()


def _top16_keys(s, key, rank):
    n, l = s.shape
    depth = n // SUBLANES
    val = [s[SUBLANES * v:SUBLANES * (v + 1), :] for v in range(depth)]
    pay = [key[SUBLANES * v:SUBLANES * (v + 1), :] for v in range(depth)]
    for a, b in _SORT16:
        swap = (val[b] > val[a]) | ((val[b] == val[a]) & (pay[b] < pay[a]))
        val[a], val[b] = jnp.where(swap, val[b], val[a]), jnp.where(swap, val[a], val[b])
        pay[a], pay[b] = jnp.where(swap, pay[b], pay[a]), jnp.where(swap, pay[a], pay[b])
    vals = jnp.zeros((PEER_TOPK, l), F32)
    idxs = jnp.zeros((PEER_TOPK, l), F32)
    for r in range(PEER_TOPK):
        mx = jnp.max(val[0], axis=0, keepdims=True)
        idx = jnp.min(jnp.where(val[0] == mx, pay[0], float(n)), axis=0, keepdims=True)
        hit = pay[0] == idx
        vals = _acc_row(vals, rank, float(r), mx)
        idxs = _acc_row(idxs, rank, float(r), idx)
        for p in range(PEER_TOPK - 1 - r):
            val[p] = jnp.where(hit, val[p + 1], val[p])
            pay[p] = jnp.where(hit, pay[p + 1], pay[p])
    return vals, idxs


def _top16_pairs(s0, i0, s1, i1, rank):
    l = s0.shape[1]
    sub = lax.broadcasted_iota(jnp.int32, (SUBLANES, l), 0)
    subf = sub.astype(F32)
    lists = []
    for p in range(PEER_TOPK):
        plane = s0[0:SUBLANES] + s1[p:p + 1]
        n_valid = PEER_TOPK // (p + 1)
        if n_valid < SUBLANES:
            plane = jnp.where(sub < n_valid, plane, NEG_INF)
        lists.append(plane)
    tail = s0[SUBLANES:] + s1[0:1]
    pos_tail = (subf + float(SUBLANES)) * float(PEER_TOPK)
    pops = jnp.zeros((SUBLANES, l), F32)
    no_pos = float(PEER_TOPK * PEER_TOPK)
    vals = jnp.zeros((PEER_TOPK, l), F32)
    poss = jnp.zeros((PEER_TOPK, l), F32)
    for r in range(PEER_TOPK):
        pos_head = subf * float(PEER_TOPK) + pops
        mx = jnp.max(jnp.maximum(lists[0], tail), axis=0, keepdims=True)
        cand = jnp.minimum(jnp.where(lists[0] == mx, pos_head, no_pos),
                           jnp.where(tail == mx, pos_tail, no_pos))
        pos = jnp.min(cand, axis=0, keepdims=True)
        hit = pos_head == pos
        vals = _acc_row(vals, rank, float(r), mx)
        poss = _acc_row(poss, rank, float(r), pos)
        pops = pops + jnp.where(hit, 1.0, 0.0)
        for p in range(PEER_TOPK - 1 - r):
            lists[p] = jnp.where(hit, lists[p + 1], lists[p])
        tail = jnp.where(pos_tail == pos, NEG_INF, tail)
    pi = jnp.floor(poss * (1.0 / PEER_TOPK))
    pj = poss - pi * float(PEER_TOPK)
    ia = jnp.zeros((PEER_TOPK, l), F32)
    ib = jnp.zeros((PEER_TOPK, l), F32)
    for k in range(PEER_TOPK):
        ia = jnp.where(pi == float(k), i0[k:k + 1], ia)
        ib = jnp.where(pj == float(k), i1[k:k + 1], ib)
    return vals, ia, ib


W_PITCH = 136


def _peer_route_kernel(h_ref, wq_ref, sk_ref, w_ref, hb_ref,
                       iat_s, ibt_s, gt_s, ia_s, ib_s, ghi_s, glo_s, wbuf, *, tt):

    @pl.when(pl.program_id(0) == 0)
    def _():
        for ref in (ia_s, ib_s, ghi_s, glo_s):
            ref[...] = jnp.zeros_like(ref)

    hb_ref[...] = h_ref[...].astype(BF16)

    def one_head(hd, key, rank):
        cols = slice(hd * 2 * PEER_HALF, (hd + 1) * 2 * PEER_HALF)
        q = jnp.dot(hb_ref[...], wq_ref[:, cols], preferred_element_type=F32).astype(BF16)
        tops = []
        for p in range(2):
            s = lax.dot_general(sk_ref[hd, p], q[:, p * PEER_HALF:(p + 1) * PEER_HALF], _NT,
                                preferred_element_type=F32)
            tops.append(_top16_keys(s, key, rank))
        (s0, i0), (s1, i1) = tops
        best, ia, ib = _top16_pairs(s0, i0, s1, i1, rank)
        e = jnp.exp(best - best[0:1])
        g = e / jnp.sum(e, axis=0, keepdims=True)
        rows = pl.ds(pl.multiple_of(hd * PEER_TOPK, PEER_TOPK), PEER_TOPK)
        iat_s[rows, :] = ia
        ibt_s[rows, :] = ib
        gt_s[rows, :] = g

    onehot_key = lax.broadcasted_iota(jnp.int32, (PEER_KEYS, LANES), 0).astype(F32).astype(BF16)
    zero = jnp.zeros((PEER_KEYS, LANES), BF16)
    one = jnp.ones((PEER_KEYS, LANES), BF16)
    pack_rows = 2 * SUBLANES

    def row_plane(ref, t):
        tile = jnp.broadcast_to(ref[t:t + 1, :], (pack_rows, LANES)).astype(BF16)
        return jnp.concatenate([tile] * (PEER_KEYS // pack_rows), axis=0)

    def scatter_token(t):
        hit_a = row_plane(ia_s, t) == onehot_key
        hit_b = row_plane(ib_s, t) == onehot_key
        a_hi = jnp.where(hit_a, row_plane(ghi_s, t), zero)
        a_lo = jnp.where(hit_a, row_plane(glo_s, t), zero)
        b_one = jnp.where(hit_b, one, zero)
        lhs = jnp.concatenate([a_hi, a_lo], axis=1)
        rhs = jnp.concatenate([b_one, b_one], axis=1)
        w = lax.dot_general(lhs, rhs, _NT, preferred_element_type=F32)
        wbuf[pl.ds(t, PEER_KEYS, stride=W_PITCH), :] = w

    key = lax.broadcasted_iota(jnp.int32, (PEER_KEYS, tt), 0).astype(F32)
    rank = lax.broadcasted_iota(jnp.int32, (PEER_TOPK, tt), 0).astype(F32)
    per_head = tt // PEER_HEADS
    for hd in range(PEER_HEADS):
        one_head(hd, key, rank)
        for t in range(hd * per_head, (hd + 1) * per_head):
            scatter_token(t)
    for ia in range(PEER_KEYS):
        w_ref[ia] = wbuf[ia * W_PITCH:ia * W_PITCH + tt, :]

    ia_s[...] = iat_s[...].T
    ib_s[...] = ibt_s[...].T
    g = gt_s[...].T
    g_hi = g.astype(BF16).astype(F32)
    ghi_s[...] = g_hi
    glo_s[...] = g - g_hi


def _peer_route(h2, wq, sk):
    t, d = h2.shape
    tt = 128
    n = t // tt
    kern = functools.partial(_peer_route_kernel, tt=tt)
    cspec = lambda a: pl.BlockSpec(a.shape, lambda i: (0,) * a.ndim)
    slab = pltpu.VMEM((PEER_HEADS * PEER_TOPK, tt), F32)
    tok = pltpu.VMEM((tt, PEER_HEADS * PEER_TOPK), F32)
    cur = lambda i: (jnp.minimum(i, n - 1), 0)
    return pl.pallas_call(
        kern,
        grid=(n + 1,),
        in_specs=[pl.BlockSpec((tt, d), cur),
                  cspec(wq), cspec(sk)],
        out_specs=[pl.BlockSpec((PEER_KEYS, tt, PEER_KEYS),
                                lambda i: (0, jnp.maximum(i - 1, 0), 0)),
                   pl.BlockSpec((tt, d), cur)],
        out_shape=[jax.ShapeDtypeStruct((PEER_KEYS, t, PEER_KEYS), F32),
                   jax.ShapeDtypeStruct((t, d), BF16)],
        scratch_shapes=[slab, slab, slab, tok, tok, tok, tok,
                        pltpu.VMEM((PEER_KEYS * W_PITCH, PEER_KEYS), F32)],
        compiler_params=_cparams(1),
        name="peer_route",
    )(h2, wq, sk)


def _gelu(z):
    return 0.5 * z * (1.0 + lax.erf(z * (1.0 / math.sqrt(2.0))))


def _peer_dense_kernel(h_ref, w_ref, u_ref, v_ref, x_ref, mod_ref, o_ref, acc_ref, *, n_ia):
    j = pl.program_id(1)

    @pl.when(j == 0)
    def _():
        acc_ref[...] = jnp.zeros_like(acc_ref)

    z = lax.dot_general(h_ref[...], u_ref[...], _NT, preferred_element_type=F32)
    w = jnp.concatenate([w_ref[i] for i in range(n_ia)], axis=1)
    a = (_gelu(z) * w).astype(BF16)
    acc_ref[...] += jnp.dot(a, v_ref[...], preferred_element_type=F32)

    @pl.when(j == pl.num_programs(1) - 1)
    def _():
        o_ref[...] = x_ref[...] + mod_ref[0][5:6] * acc_ref[...]


def _peer_dense(h_bf, w3, u_all, v_all, layer, x, mod, seq):
    t, d = h_bf.shape
    tt = min(1024, seq)
    n_ia = SUBLANES
    e_blk = n_ia * PEER_KEYS
    assert PEER_EXPERTS // e_blk > 1
    kern = functools.partial(_peer_dense_kernel, n_ia=n_ia)
    return pl.pallas_call(
        kern,
        grid=(t // tt, PEER_EXPERTS // e_blk),
        in_specs=[
            pl.BlockSpec((tt, d), lambda i, j: (i, 0)),
            pl.BlockSpec((n_ia, tt, PEER_KEYS), lambda i, j: (j, i, 0)),
            pl.BlockSpec((None, e_blk, d), lambda i, j: (layer, j, 0)),
            pl.BlockSpec((None, e_blk, d), lambda i, j: (layer, j, 0)),
            pl.BlockSpec((tt, d), lambda i, j: (jnp.where(j == 0, jnp.maximum(i - 1, 0), i), 0)),
            pl.BlockSpec((1, 6, d), lambda i, j: ((i * tt) // seq, 0, 0)),
        ],
        out_specs=pl.BlockSpec((tt, d), lambda i, j: (i, 0)),
        out_shape=jax.ShapeDtypeStruct((t, d), F32),
        scratch_shapes=[pltpu.VMEM((tt, d), F32)],
        compiler_params=_cparams(2),
        name="peer_dense",
    )(h_bf, w3, u_all, v_all, x, mod)


def _peer(h2, x, mod, w_q, subkeys, u_all, v_all, layer):
    b, s, d = x.shape
    h2f = h2.reshape(b * s, d)
    w3, h_bf = _peer_route(h2f, w_q.astype(BF16), subkeys.astype(BF16))
    out = _peer_dense(h_bf, w3, u_all, v_all, layer, x.reshape(b * s, d), mod, s)
    return out.reshape(b, s, d)


def kernel(x, c, positions, tok_norm, ch_norm, ada_w, ada_b, conv_in_w, conv_w, conv_out_w,
           mla_in_w, q_a_norm, kv_a_norm, q_b_w, kv_b_w, q_norm, k_norm, mla_out_w,
           peer_q_w, peer_subkeys, peer_u, peer_v):
    b, s, d = x.shape
    depth = ada_w.shape[0]
    bp = -(-b // SUBLANES) * SUBLANES
    c_pad = jnp.zeros((bp, d), F32).at[:b].set(c)
    mod_all = _ada(c_pad, ada_w, ada_b)[:, :b].reshape(depth, b, 6, d)

    u_all = peer_u.astype(BF16)
    v_all = peer_v.astype(BF16)
    row = lambda a: a.reshape(1, -1)
    for i in range(depth):
        mod = mod_all[i]
        j = i // 2
        if i % 2 == 0:
            x, h2 = _conv_layer(x, mod, row(tok_norm[i]), row(ch_norm[i]),
                                conv_in_w[j].astype(BF16), conv_w[j],
                                conv_out_w[j].astype(BF16))
        else:
            w_in = mla_in_w[j].astype(BF16)
            wq = w_in[:, :Q_LORA]
            wkv = w_in[:, Q_LORA:Q_LORA + KV_LORA]
            wkr = jnp.tile(w_in[:, Q_LORA + KV_LORA:], (1, LANES // QK_ROPE))
            wqb = q_b_w[j].astype(BF16).reshape(Q_LORA, MLA_HEADS, QK_HEAD)
            wqbn = wqb[:, :, :QK_NOPE].reshape(Q_LORA, MLA_HEADS * QK_NOPE)
            wqbr = wqb[:, :, QK_NOPE:].reshape(Q_LORA, MLA_HEADS * QK_ROPE)
            inv_freq = ROPE_THETA ** (-jnp.arange(0, QK_ROPE, 2, dtype=F32) / QK_ROPE)
            blocks = LANES // QK_ROPE
            half = QK_ROPE // 2
            freq = jnp.tile(inv_freq, 2 * blocks).reshape(1, LANES)
            sgn = jnp.tile(jnp.concatenate([-jnp.ones((half,), F32), jnp.ones((half,), F32)]),
                           blocks).reshape(1, LANES)
            q, k, v = _mla_proj(
                x, mod, row(tok_norm[i]), positions.reshape(b, s, 1), freq, sgn,
                wq, wkv, wkr, row(q_a_norm[j]), row(kv_a_norm[j]),
                wqbn, wqbr, kv_b_w[j].astype(BF16),
                row(q_norm[j][:QK_NOPE]), row(jnp.tile(q_norm[j][QK_NOPE:], blocks)),
                row(k_norm[j][:QK_NOPE]), row(jnp.tile(k_norm[j][QK_NOPE:], blocks)))
            o = _flash(q, k, v)
            x, h2 = _mla_out(o, x, mod, row(ch_norm[i]), mla_out_w[j].astype(BF16))
        x = _peer(h2, x, mod, peer_q_w[i], peer_subkeys[i], u_all, v_all, i)
    return x
```

```python
import functools
import math

import jax
import jax.numpy as jnp
from jax import lax
from jax.experimental import pallas as pl
from jax.experimental.pallas import tpu as pltpu

F32 = jnp.float32
BF16 = jnp.bfloat16

D_MODEL = 1024
MLA_HEADS = 8
Q_LORA = 384
KV_LORA = 256
QK_NOPE = 128
QK_ROPE = 64
QK_HEAD = QK_NOPE + QK_ROPE
QK_PAD = 256
V_HEAD = 128
ROPE_THETA = 10000.0
PEER_HEADS = 8
PEER_KEYS = 128
PEER_EXPERTS = PEER_KEYS * PEER_KEYS
PEER_HALF = 128
PEER_TOPK = 16
NORM_EPS = 1e-6

LANES = 128
SUBLANES = 8
VMEM_LIMIT = 56 * 1024 * 1024

SEQ_TILE = 512
ROUTE_TILE = 128
DENSE_TILE = 1024
DENSE_SLABS = SUBLANES

NEG_INF = float("-inf")
_NT = (((1,), (1,)), ((), ()))


def _cparams(n_axes):
    return pltpu.CompilerParams(
        dimension_semantics=("arbitrary",) * n_axes, vmem_limit_bytes=VMEM_LIMIT)


def _table_cast_specs(table, layer, n_steps, step_index):
    _, n, d = table.shape
    rows = n // n_steps
    assert rows * n_steps == n and rows % (2 * SUBLANES) == 0
    in_spec = pl.BlockSpec((None, rows, d), lambda *g: (layer, step_index(*g), 0))
    out_spec = pl.BlockSpec((rows, d), lambda *g: (step_index(*g), 0))
    return in_spec, out_spec, jax.ShapeDtypeStruct((n, d), BF16)


def _modulate(x, g, shift, scale):
    ms = jnp.mean(x * x, axis=-1, keepdims=True)
    y = x * lax.rsqrt(ms + NORM_EPS)
    return (y * g) * (1.0 + scale) + shift


def _ada_kernel(c_ref, w_ref, b_ref, o_ref):
    c = c_ref[...]
    ca = c * jax.nn.sigmoid(c)
    o_ref[0] = jnp.dot(ca, w_ref[0], preferred_element_type=F32,
                       precision=lax.Precision.HIGHEST) + b_ref[0]


def _ada(c_pad, ada_w, ada_b):
    depth, d, d6 = ada_w.shape
    bp = c_pad.shape[0]
    nb = d6 // d
    return pl.pallas_call(
        _ada_kernel,
        grid=(depth, nb),
        in_specs=[
            pl.BlockSpec((bp, d), lambda i, j: (0, 0)),
            pl.BlockSpec((1, d, d), lambda i, j: (i, 0, j)),
            pl.BlockSpec((1, 1, d), lambda i, j: (i, 0, j)),
        ],
        out_specs=pl.BlockSpec((1, bp, d), lambda i, j: (i, 0, j)),
        out_shape=jax.ShapeDtypeStruct((depth, bp, d6), F32),
        compiler_params=_cparams(2),
        name="ada",
    )(c_pad, ada_w, ada_b.reshape(depth, 1, d6))


def _conv_layer_kernel(x_ref, mod_ref, tn_ref, cn_ref, win_ref, cw_ref, wout_ref, uf_ref, vf_ref,
                       x1_ref, h2_ref, ub_ref, vb_ref, ubuf, *, ts):
    ub_ref[...] = uf_ref[...].astype(BF16)
    vb_ref[...] = vf_ref[...].astype(BF16)

    d = D_MODEL
    x = x_ref[0]
    m = mod_ref[0]
    sh1, sc1, g1, sh2, sc2 = m[0:1], m[1:2], m[2:3], m[3:4], m[4:5]
    h = _modulate(x, tn_ref[...], sh1, sc1)
    bcv = jnp.dot(h.astype(BF16), win_ref[...], preferred_element_type=F32)
    b_gate = bcv[:, :d]
    u = bcv[:, d:2 * d] * bcv[:, 2 * d:]

    @pl.when(pl.program_id(1) == 0)
    def _():
        ubuf[0:SUBLANES, :] = jnp.zeros((SUBLANES, d), F32)

    ubuf[SUBLANES:SUBLANES + ts, :] = u
    u1 = ubuf[SUBLANES - 1:SUBLANES - 1 + ts, :]
    u2 = ubuf[SUBLANES - 2:SUBLANES - 2 + ts, :]
    cw = cw_ref[...]
    conv = cw[0:1] * u2 + cw[1:2] * u1 + cw[2:3] * u
    ubuf[0:SUBLANES, :] = u[ts - SUBLANES:, :]
    y = jnp.dot((b_gate * conv).astype(BF16), wout_ref[...], preferred_element_type=F32)
    x1 = x + g1 * y
    x1_ref[0] = x1
    h2_ref[0] = _modulate(x1, cn_ref[...], sh2, sc2)


def _conv_layer(x, mod, tok_norm, ch_norm, w_in, conv_w, w_out, u_tab, v_tab, layer):
    b, s, d = x.shape
    ts = min(SEQ_TILE, s)
    nj = s // ts
    kern = functools.partial(_conv_layer_kernel, ts=ts)
    const = lambda *shape: pl.BlockSpec(shape, lambda i, j: (0,) * len(shape))
    step = lambda i, j: i * nj + j
    u_in, u_out, u_shape = _table_cast_specs(u_tab, layer, b * nj, step)
    v_in, v_out, v_shape = _table_cast_specs(v_tab, layer, b * nj, step)
    return pl.pallas_call(
        kern,
        grid=(b, nj),
        in_specs=[
            pl.BlockSpec((1, ts, d), lambda i, j: (i, j, 0)),
            pl.BlockSpec((1, 6, d), lambda i, j: (i, 0, 0)),
            const(1, d), const(1, d), const(d, 3 * d), const(3, d), const(d, d), u_in, v_in,
        ],
        out_specs=[pl.BlockSpec((1, ts, d), lambda i, j: (i, j, 0))] * 2 + [u_out, v_out],
        out_shape=[jax.ShapeDtypeStruct((b, s, d), F32)] * 2 + [u_shape, v_shape],
        scratch_shapes=[pltpu.VMEM((ts + SUBLANES, d), F32)],
        compiler_params=_cparams(2),
        name="conv_layer",
    )(x, mod, tok_norm, ch_norm, w_in, conv_w, w_out, u_tab, v_tab)


def _mla_proj_kernel(x_ref, mod_ref, tn_ref, pos_ref, freq_ref, sgn_ref,
                     wq_ref, wkv_ref, wkr_ref, qag_ref, kvag_ref,
                     wqbn_ref, wqbr_ref, wkvb_ref, qgn_ref, qgr_ref, kgn_ref, kgr_ref,
                     q_ref, k_ref, v_ref, *, ts):
    x = x_ref[0]
    m = mod_ref[0]
    h = _modulate(x, tn_ref[...], m[0:1], m[1:2]).astype(BF16)
    q_lat = jnp.dot(h, wq_ref[...], preferred_element_type=F32)
    kv_lat = jnp.dot(h, wkv_ref[...], preferred_element_type=F32)
    k_rope = jnp.dot(h, wkr_ref[...], preferred_element_type=F32)

    def rms(v, g):
        return v * lax.rsqrt(jnp.mean(v * v, axis=-1, keepdims=True) + NORM_EPS) * g

    qn = rms(q_lat, qag_ref[...]).astype(BF16)
    kvn = rms(kv_lat, kvag_ref[...]).astype(BF16)
    q_nope = jnp.dot(qn, wqbn_ref[...], preferred_element_type=F32)
    q_rope = jnp.dot(qn, wqbr_ref[...], preferred_element_type=F32)
    kv = jnp.dot(kvn, wkvb_ref[...], preferred_element_type=F32)

    ang = pos_ref[0].astype(F32) * freq_ref[...]
    cos = jnp.cos(ang)
    sin_signed = jnp.sin(ang) * sgn_ref[...]
    lane = lax.broadcasted_iota(jnp.int32, (ts, LANES), 1)
    first_half = (lane % QK_ROPE) < QK_ROPE // 2
    block_of = [lane < QK_ROPE, lane >= QK_ROPE]

    def rope(v):
        rot = jnp.where(first_half, pltpu.roll(v, LANES - QK_ROPE // 2, axis=1),
                        pltpu.roll(v, QK_ROPE // 2, axis=1))
        return v * cos + rot * sin_signed

    scale = QK_HEAD ** -0.5 * math.log2(math.e)
    kr_g = rope(k_rope * kgr_ref[...])
    kr_ss = 0.5 * jnp.sum(k_rope * k_rope, axis=-1, keepdims=True)
    pair_raw, pair_g = None, None
    for hd in range(MLA_HEADS):
        own = block_of[hd % 2]
        if hd % 2 == 0:
            pair_raw = q_rope[:, (hd // 2) * LANES:(hd // 2 + 1) * LANES]
            pair_g = rope(pair_raw * qgr_ref[...])
        qn_h = q_nope[:, hd * QK_NOPE:(hd + 1) * QK_NOPE]
        qr_h = jnp.where(own, pair_raw, 0.0)
        ss = jnp.sum(qn_h * qn_h + qr_h * qr_h, axis=-1, keepdims=True)
        rinv = lax.rsqrt(ss * (1.0 / QK_HEAD) + NORM_EPS) * scale
        qa = qn_h * rinv * qgn_ref[...]
        qb = jnp.where(own, pair_g, 0.0) * rinv
        q_ref[0, hd] = jnp.concatenate([qa, qb], axis=1).astype(BF16)

        kn_h = kv[:, hd * 2 * QK_NOPE:hd * 2 * QK_NOPE + QK_NOPE]
        v_h = kv[:, hd * 2 * QK_NOPE + QK_NOPE:(hd + 1) * 2 * QK_NOPE]
        kss = jnp.sum(kn_h * kn_h, axis=-1, keepdims=True) + kr_ss
        krinv = lax.rsqrt(kss * (1.0 / QK_HEAD) + NORM_EPS)
        ka = kn_h * krinv * kgn_ref[...]
        kb = jnp.where(own, kr_g, 0.0) * krinv
        k_ref[0, hd] = jnp.concatenate([ka, kb], axis=1).astype(BF16)
        v_ref[0, hd] = v_h.astype(BF16)


def _mla_proj(x, mod, tok_norm, pos3, freq, sgn, wq, wkv, wkr, qag, kvag,
              wqbn, wqbr, wkvb, qgn, qgr, kgn, kgr):
    b, s, d = x.shape
    ts = min(SEQ_TILE, s)
    kern = functools.partial(_mla_proj_kernel, ts=ts)
    consts = [tok_norm]
    rest = [freq, sgn, wq, wkv, wkr, qag, kvag, wqbn, wqbr, wkvb, qgn, qgr, kgn, kgr]
    cspec = lambda a: pl.BlockSpec(a.shape, lambda i, j: (0,) * a.ndim)
    hq = pl.BlockSpec((1, MLA_HEADS, ts, QK_PAD), lambda i, j: (i, 0, j, 0))
    hv = pl.BlockSpec((1, MLA_HEADS, ts, V_HEAD), lambda i, j: (i, 0, j, 0))
    return pl.pallas_call(
        kern,
        grid=(b, s // ts),
        in_specs=[
            pl.BlockSpec((1, ts, d), lambda i, j: (i, j, 0)),
            pl.BlockSpec((1, 6, d), lambda i, j: (i, 0, 0)),
            cspec(tok_norm),
            pl.BlockSpec((1, ts, 1), lambda i, j: (i, j, 0)),
        ] + [cspec(a) for a in rest],
        out_specs=[hq, hq, hv],
        out_shape=[
            jax.ShapeDtypeStruct((b, MLA_HEADS, s, QK_PAD), BF16),
            jax.ShapeDtypeStruct((b, MLA_HEADS, s, QK_PAD), BF16),
            jax.ShapeDtypeStruct((b, MLA_HEADS, s, V_HEAD), BF16),
        ],
        compiler_params=_cparams(2),
        name="mla_proj",
    )(x, mod, tok_norm, pos3, *rest)


FLASH_HEADS = 4


def _flash_kernel(q_ref, k_ref, v_ref, o_ref, *, tq):
    i = pl.program_id(2)

    def block(j, carries, diagonal):
        start = pl.multiple_of(j * tq, tq)
        out = []
        for hh in range(FLASH_HEADS):
            m, l, acc = carries[hh]
            kj = k_ref[0, hh, pl.ds(start, tq), :]
            vj = v_ref[0, hh, pl.ds(start, tq), :]
            s = lax.dot_general(q_ref[0, hh], kj, _NT, preferred_element_type=F32)
            if diagonal:
                row = lax.broadcasted_iota(jnp.int32, (tq, tq), 0)
                col = lax.broadcasted_iota(jnp.int32, (tq, tq), 1)
                s = jnp.where(row >= col, s, NEG_INF)
            m_new = jnp.maximum(m, jnp.max(s, axis=-1, keepdims=True))
            p = jnp.exp2(s - m_new)
            alpha = jnp.exp2(m - m_new)
            l = alpha * l + jnp.sum(p, axis=-1, keepdims=True)
            acc = alpha * acc + jnp.dot(p.astype(BF16), vj, preferred_element_type=F32)
            out.append((m_new, l, acc))
        return tuple(out)

    init = tuple((jnp.full((tq, 1), NEG_INF, F32), jnp.zeros((tq, 1), F32),
                  jnp.zeros((tq, V_HEAD), F32)) for _ in range(FLASH_HEADS))
    carries = lax.fori_loop(0, i, lambda j, c: block(j, c, False), init)
    carries = block(i, carries, True)
    o_ref[0] = jnp.concatenate([(acc / l).astype(BF16) for _, l, acc in carries], axis=1)


def _flash(q, k, v):
    b, hds, s, _ = q.shape
    tq = min(SEQ_TILE, s)
    kern = functools.partial(_flash_kernel, tq=tq)
    fh = FLASH_HEADS
    return pl.pallas_call(
        kern,
        grid=(b, hds // fh, s // tq),
        in_specs=[
            pl.BlockSpec((1, fh, tq, QK_PAD), lambda bi, h, i: (bi, h, i, 0)),
            pl.BlockSpec((1, fh, s, QK_PAD), lambda bi, h, i: (bi, h, 0, 0)),
            pl.BlockSpec((1, fh, s, V_HEAD), lambda bi, h, i: (bi, h, 0, 0)),
        ],
        out_specs=pl.BlockSpec((1, tq, fh * V_HEAD), lambda bi, h, i: (bi, i, h)),
        out_shape=jax.ShapeDtypeStruct((b, s, hds * V_HEAD), BF16),
        compiler_params=_cparams(3),
        name="mla_flash",
    )(q, k, v)


def _mla_out_kernel(o_ref, x_ref, mod_ref, cn_ref, w_ref, x1_ref, h2_ref):
    m = mod_ref[0]
    y = jnp.dot(o_ref[0], w_ref[...], preferred_element_type=F32)
    x1 = x_ref[0] + m[2:3] * y
    x1_ref[0] = x1
    h2_ref[0] = _modulate(x1, cn_ref[...], m[3:4], m[4:5])


def _mla_out(o, x, mod, ch_norm, w_out):
    b, s, d = x.shape
    ts = min(SEQ_TILE, s)
    return pl.pallas_call(
        _mla_out_kernel,
        grid=(b, s // ts),
        in_specs=[
            pl.BlockSpec((1, ts, d), lambda i, j: (i, j, 0)),
            pl.BlockSpec((1, ts, d), lambda i, j: (i, j, 0)),
            pl.BlockSpec((1, 6, d), lambda i, j: (i, 0, 0)),
            pl.BlockSpec((1, d), lambda i, j: (0, 0)),
            pl.BlockSpec((d, d), lambda i, j: (0, 0)),
        ],
        out_specs=[pl.BlockSpec((1, ts, d), lambda i, j: (i, j, 0))] * 2,
        out_shape=[jax.ShapeDtypeStruct((b, s, d), F32)] * 2,
        compiler_params=_cparams(2),
        name="mla_out",
    )(o, x, mod, ch_norm, w_out)


def _sort16_network():
    def merge(lo, hi, r):
        step = r * 2
        if step < hi - lo:
            yield from merge(lo, hi, step)
            yield from merge(lo + r, hi, step)
            yield from ((i, i + r) for i in range(lo + r, hi - r, step))
        else:
            yield (lo, lo + r)

    def sort(lo, hi):
        if hi - lo >= 1:
            mid = lo + (hi - lo) // 2
            yield from sort(lo, mid)
            yield from sort(mid + 1, hi)
            yield from merge(lo, hi, 1)

    return tuple(sort(0, PEER_TOPK - 1))


_SORT16 = _sort16_network()


def _top16_keys(s, key, stack):
    n, l = s.shape
    depth = n // SUBLANES
    val = [s[SUBLANES * v:SUBLANES * (v + 1), :] for v in range(depth)]
    pay = [key[SUBLANES * v:SUBLANES * (v + 1), :] for v in range(depth)]
    for a, b in _SORT16:
        swap = (val[b] > val[a]) | ((val[b] == val[a]) & (pay[b] < pay[a]))
        val[a], val[b] = jnp.where(swap, val[b], val[a]), jnp.where(swap, val[a], val[b])
        pay[a], pay[b] = jnp.where(swap, pay[b], pay[a]), jnp.where(swap, pay[a], pay[b])
    vals, idxs = [], []
    for r in range(PEER_TOPK):
        mx = jnp.max(val[0], axis=0, keepdims=True)
        idx = jnp.min(jnp.where(val[0] == mx, pay[0], float(n)), axis=0, keepdims=True)
        hit = pay[0] == idx
        vals.append(mx)
        idxs.append(idx)
        for p in range(PEER_TOPK - 1 - r):
            val[p] = jnp.where(hit, val[p + 1], val[p])
            pay[p] = jnp.where(hit, pay[p + 1], pay[p])
    return stack(vals), stack(idxs)


def _top16_pairs(s0, i0, s1, i1, stack):
    l = s0.shape[1]
    sub = lax.broadcasted_iota(jnp.int32, (SUBLANES, l), 0)
    subf = sub.astype(F32)
    lists = []
    for p in range(PEER_TOPK):
        plane = s0[0:SUBLANES] + s1[p:p + 1]
        n_valid = PEER_TOPK // (p + 1)
        if n_valid < SUBLANES:
            plane = jnp.where(sub < n_valid, plane, NEG_INF)
        lists.append(plane)
    tail = s0[SUBLANES:] + s1[0:1]
    pos_tail = (subf + float(SUBLANES)) * float(PEER_TOPK)
    pops = jnp.zeros((SUBLANES, l), F32)
    no_pos = float(PEER_TOPK * PEER_TOPK)
    vals, poss = [], []
    for r in range(PEER_TOPK):
        pos_head = subf * float(PEER_TOPK) + pops
        mx = jnp.max(jnp.maximum(lists[0], tail), axis=0, keepdims=True)
        cand = jnp.minimum(jnp.where(lists[0] == mx, pos_head, no_pos),
                           jnp.where(tail == mx, pos_tail, no_pos))
        pos = jnp.min(cand, axis=0, keepdims=True)
        hit = pos_head == pos
        vals.append(mx)
        poss.append(pos)
        pops = pops + jnp.where(hit, 1.0, 0.0)
        for p in range(PEER_TOPK - 1 - r):
            lists[p] = jnp.where(hit, lists[p + 1], lists[p])
        tail = jnp.where(pos_tail == pos, NEG_INF, tail)
    vals, poss = stack(vals), stack(poss)
    pi = jnp.floor(poss * (1.0 / PEER_TOPK))
    pj = poss - pi * float(PEER_TOPK)
    ia = jnp.zeros((PEER_TOPK, l), F32)
    ib = jnp.zeros((PEER_TOPK, l), F32)
    for k in range(PEER_TOPK):
        ia = jnp.where(pi == float(k), i0[k:k + 1], ia)
        ib = jnp.where(pj == float(k), i1[k:k + 1], ib)
    return vals, ia, ib


W_PITCH = 136


def _peer_route_kernel(h_ref, wq_ref, sk_ref, w_ref, hb_ref,
                       rows_s, iat_s, ibt_s, gt_s, ia_s, ib_s, ghi_s, glo_s, wbuf, *, tt):

    @pl.when(pl.program_id(0) == 0)
    def _():
        for ref in (ia_s, ib_s, ghi_s, glo_s):
            ref[...] = jnp.zeros_like(ref)

    hb_ref[...] = h_ref[...].astype(BF16)

    def one_head(hd, key):
        planes = iter(range(rows_s.shape[1]))

        def stack(rows):
            plane = rows_s.at[hd, next(planes)]
            for r, row in enumerate(rows):
                plane[r:r + 1, :] = row
            return plane[...]

        cols = slice(hd * 2 * PEER_HALF, (hd + 1) * 2 * PEER_HALF)
        q = jnp.dot(hb_ref[...], wq_ref[:, cols], preferred_element_type=F32).astype(BF16)
        tops = []
        for p in range(2):
            s = lax.dot_general(sk_ref[hd, p], q[:, p * PEER_HALF:(p + 1) * PEER_HALF], _NT,
                                preferred_element_type=F32)
            tops.append(_top16_keys(s, key, stack))
        (s0, i0), (s1, i1) = tops
        best, ia, ib = _top16_pairs(s0, i0, s1, i1, stack)
        e = jnp.exp(best - best[0:1])
        g = e / jnp.sum(e, axis=0, keepdims=True)
        rows = pl.ds(pl.multiple_of(hd * PEER_TOPK, PEER_TOPK), PEER_TOPK)
        iat_s[rows, :] = ia
        ibt_s[rows, :] = ib
        gt_s[rows, :] = g

    onehot_key = lax.broadcasted_iota(jnp.int32, (PEER_KEYS, LANES), 0).astype(F32).astype(BF16)
    zero = jnp.zeros((PEER_KEYS, LANES), BF16)
    one = jnp.ones((PEER_KEYS, LANES), BF16)
    pack_rows = 2 * SUBLANES

    def row_plane(ref, t):
        tile = jnp.broadcast_to(ref[t:t + 1, :], (pack_rows, LANES)).astype(BF16)
        return jnp.concatenate([tile] * (PEER_KEYS // pack_rows), axis=0)

    def scatter_token(t):
        hit_a = row_plane(ia_s, t) == onehot_key
        hit_b = row_plane(ib_s, t) == onehot_key
        a_hi = jnp.where(hit_a, row_plane(ghi_s, t), zero)
        a_lo = jnp.where(hit_a, row_plane(glo_s, t), zero)
        b_one = jnp.where(hit_b, one, zero)
        lhs = jnp.concatenate([a_hi, a_lo], axis=1)
        rhs = jnp.concatenate([b_one, b_one], axis=1)
        w = lax.dot_general(lhs, rhs, _NT, preferred_element_type=F32)
        wbuf[pl.ds(t, PEER_KEYS, stride=W_PITCH), :] = w

    key = lax.broadcasted_iota(jnp.int32, (PEER_KEYS, tt), 0).astype(F32)
    per_head = tt // PEER_HEADS
    for hd in range(PEER_HEADS):
        one_head(hd, key)
        for t in range(hd * per_head, (hd + 1) * per_head):
            scatter_token(t)
    for ia in range(PEER_KEYS):
        w_ref[ia] = wbuf[ia * W_PITCH:ia * W_PITCH + tt, :]

    ia_s[...] = iat_s[...].T
    ib_s[...] = ibt_s[...].T
    g = gt_s[...].T
    g_hi = g.astype(BF16).astype(F32)
    ghi_s[...] = g_hi
    glo_s[...] = g - g_hi


def _peer_route(h2, wq, sk):
    t, d = h2.shape
    tt = ROUTE_TILE
    n = t // tt
    kern = functools.partial(_peer_route_kernel, tt=tt)
    cspec = lambda a: pl.BlockSpec(a.shape, lambda i: (0,) * a.ndim)
    slab = pltpu.VMEM((PEER_HEADS * PEER_TOPK, tt), F32)
    tok = pltpu.VMEM((tt, PEER_HEADS * PEER_TOPK), F32)
    cur = lambda i: (jnp.minimum(i, n - 1), 0)
    return pl.pallas_call(
        kern,
        grid=(n + 1,),
        in_specs=[pl.BlockSpec((tt, d), cur),
                  cspec(wq), cspec(sk)],
        out_specs=[pl.BlockSpec((PEER_KEYS, tt, PEER_KEYS),
                                lambda i: (0, jnp.maximum(i - 1, 0), 0)),
                   pl.BlockSpec((tt, d), cur)],
        out_shape=[jax.ShapeDtypeStruct((PEER_KEYS, t, PEER_KEYS), F32),
                   jax.ShapeDtypeStruct((t, d), BF16)],
        scratch_shapes=[pltpu.VMEM((PEER_HEADS, 6, PEER_TOPK, tt), F32), slab, slab, slab, tok, tok, tok, tok,
                        pltpu.VMEM((PEER_KEYS * W_PITCH, PEER_KEYS), F32)],
        compiler_params=_cparams(1),
        name="peer_route",
    )(h2, wq, sk)


def _gelu(z):
    return 0.5 * z * (1.0 + lax.erf(z * (1.0 / math.sqrt(2.0))))


def _peer_dense_kernel(h_ref, w_ref, u_ref, v_ref, x_ref, mod_ref, *rest, n_ia, cast_next):
    if cast_next:
        uf_ref, vf_ref, o_ref, ub_ref, vb_ref, acc_ref = rest
        ub_ref[...] = uf_ref[...].astype(BF16)
        vb_ref[...] = vf_ref[...].astype(BF16)
    else:
        o_ref, acc_ref = rest
    j = pl.program_id(1)

    @pl.when(j == 0)
    def _():
        acc_ref[...] = jnp.zeros_like(acc_ref)

    z = lax.dot_general(h_ref[...], u_ref[...], _NT, preferred_element_type=F32)
    w = jnp.concatenate([w_ref[i] for i in range(n_ia)], axis=1)
    a = (_gelu(z) * w).astype(BF16)
    acc_ref[...] += jnp.dot(a, v_ref[...], preferred_element_type=F32)

    @pl.when(j == pl.num_programs(1) - 1)
    def _():
        o_ref[...] = x_ref[...] + mod_ref[0][5:6] * acc_ref[...]


def _peer_dense(h_bf, w3, u_bf, v_bf, x, mod, seq, next_tables):
    t, d = h_bf.shape
    tt = min(DENSE_TILE, seq)
    n_ia = DENSE_SLABS
    e_blk = n_ia * PEER_KEYS
    ni, nj = t // tt, PEER_EXPERTS // e_blk
    cast_next = next_tables is not None
    kern = functools.partial(_peer_dense_kernel, n_ia=n_ia, cast_next=cast_next)
    in_specs = [
        pl.BlockSpec((tt, d), lambda i, j: (i, 0)),
        pl.BlockSpec((n_ia, tt, PEER_KEYS), lambda i, j: (j, i, 0)),
        pl.BlockSpec((e_blk, d), lambda i, j: (j, 0)),
        pl.BlockSpec((e_blk, d), lambda i, j: (j, 0)),
        pl.BlockSpec((tt, d), lambda i, j: (i, 0)),
        pl.BlockSpec((1, 6, d), lambda i, j: ((i * tt) // seq, 0, 0)),
    ]
    out_specs = [pl.BlockSpec((tt, d), lambda i, j: (i, 0))]
    out_shape = [jax.ShapeDtypeStruct((t, d), F32)]
    args = [h_bf, w3, u_bf, v_bf, x, mod]
    if cast_next:
        u_tab, v_tab, layer = next_tables
        step = lambda i, j: i * nj + j
        for tab in (u_tab, v_tab):
            t_in, t_out, t_shape = _table_cast_specs(tab, layer, ni * nj, step)
            in_specs.append(t_in)
            out_specs.append(t_out)
            out_shape.append(t_shape)
            args.append(tab)
    return pl.pallas_call(
        kern,
        grid=(ni, nj),
        in_specs=in_specs,
        out_specs=out_specs,
        out_shape=out_shape,
        scratch_shapes=[pltpu.VMEM((tt, d), F32)],
        compiler_params=_cparams(2),
        name="peer_dense",
    )(*args)


def _peer(h2, x, mod, w_q, subkeys, u_bf, v_bf, next_tables):
    b, s, d = x.shape
    h2f = h2.reshape(b * s, d)
    w3, h_bf = _peer_route(h2f, w_q.astype(BF16), subkeys.astype(BF16))
    out, *tables = _peer_dense(h_bf, w3, u_bf, v_bf, x.reshape(b * s, d), mod, s, next_tables)
    return out.reshape(b, s, d), tables


def kernel(x, c, positions, tok_norm, ch_norm, ada_w, ada_b, conv_in_w, conv_w, conv_out_w,
           mla_in_w, q_a_norm, kv_a_norm, q_b_w, kv_b_w, q_norm, k_norm, mla_out_w,
           peer_q_w, peer_subkeys, peer_u, peer_v):
    b, s, d = x.shape
    depth = ada_w.shape[0]
    bp = -(-b // SUBLANES) * SUBLANES
    c_pad = jnp.zeros((bp, d), F32).at[:b].set(c)
    mod_all = _ada(c_pad, ada_w, ada_b)[:, :b].reshape(depth, b, 6, d)

    u_bf = v_bf = None
    row = lambda a: a.reshape(1, -1)
    for i in range(depth):
        mod = mod_all[i]
        j = i // 2
        if i % 2 == 0:
            x, h2, u_now, v_now = _conv_layer(x, mod, row(tok_norm[i]), row(ch_norm[i]),
                                              conv_in_w[j].astype(BF16), conv_w[j],
                                              conv_out_w[j].astype(BF16), peer_u, peer_v, i)
            if u_bf is None:
                u_bf, v_bf = u_now, v_now
        else:
            w_in = mla_in_w[j].astype(BF16)
            wq = w_in[:, :Q_LORA]
            wkv = w_in[:, Q_LORA:Q_LORA + KV_LORA]
            wkr = jnp.tile(w_in[:, Q_LORA + KV_LORA:], (1, LANES // QK_ROPE))
            wqb = q_b_w[j].astype(BF16).reshape(Q_LORA, MLA_HEADS, QK_HEAD)
            wqbn = wqb[:, :, :QK_NOPE].reshape(Q_LORA, MLA_HEADS * QK_NOPE)
            wqbr = wqb[:, :, QK_NOPE:].reshape(Q_LORA, MLA_HEADS * QK_ROPE)
            inv_freq = ROPE_THETA ** (-jnp.arange(0, QK_ROPE, 2, dtype=F32) / QK_ROPE)
            blocks = LANES // QK_ROPE
            half = QK_ROPE // 2
            freq = jnp.tile(inv_freq, 2 * blocks).reshape(1, LANES)
            sgn = jnp.tile(jnp.concatenate([-jnp.ones((half,), F32), jnp.ones((half,), F32)]),
                           blocks).reshape(1, LANES)
            q, k, v = _mla_proj(
                x, mod, row(tok_norm[i]), positions.reshape(b, s, 1), freq, sgn,
                wq, wkv, wkr, row(q_a_norm[j]), row(kv_a_norm[j]),
                wqbn, wqbr, kv_b_w[j].astype(BF16),
                row(q_norm[j][:QK_NOPE]), row(jnp.tile(q_norm[j][QK_NOPE:], blocks)),
                row(k_norm[j][:QK_NOPE]), row(jnp.tile(k_norm[j][QK_NOPE:], blocks)))
            o = _flash(q, k, v)
            x, h2 = _mla_out(o, x, mod, row(ch_norm[i]), mla_out_w[j].astype(BF16))
        next_tables = (peer_u, peer_v, i + 1) if i + 1 < depth else None
        x, tables = _peer(h2, x, mod, peer_q_w[i], peer_subkeys[i], u_bf, v_bf, next_tables)
        if tables:
            u_bf, v_bf = tables
    return x
```

```python
import functools
import math

import jax
import jax.numpy as jnp
from jax import lax
from jax.experimental import pallas as pl
from jax.experimental.pallas import tpu as pltpu

F32 = jnp.float32
BF16 = jnp.bfloat16

D_MODEL = 1024
MLA_HEADS = 8
Q_LORA = 384
KV_LORA = 256
QK_NOPE = 128
QK_ROPE = 64
QK_HEAD = QK_NOPE + QK_ROPE
QK_PAD = 256
V_HEAD = 128
ROPE_THETA = 10000.0
PEER_HEADS = 8
PEER_KEYS = 128
PEER_EXPERTS = PEER_KEYS * PEER_KEYS
PEER_HALF = 128
PEER_TOPK = 16
NORM_EPS = 1e-6

LANES = 128
SUBLANES = 8
VMEM_LIMIT = 56 * 1024 * 1024

SEQ_TILE = 512
ROUTE_TILE = 128
DENSE_TILE = 1024
DENSE_SLABS = SUBLANES

NEG_INF = float("-inf")
_NT = (((1,), (1,)), ((), ()))


def _cparams(n_axes):
    return pltpu.CompilerParams(
        dimension_semantics=("arbitrary",) * n_axes, vmem_limit_bytes=VMEM_LIMIT)


def _table_cast_specs(table, layer, n_steps, step_index):
    _, n, d = table.shape
    rows = n // n_steps
    assert rows * n_steps == n and rows % (2 * SUBLANES) == 0
    in_spec = pl.BlockSpec((None, rows, d), lambda *g: (layer, step_index(*g), 0))
    out_spec = pl.BlockSpec((rows, d), lambda *g: (step_index(*g), 0))
    return in_spec, out_spec, jax.ShapeDtypeStruct((n, d), BF16)


def _modulate(x, g, shift, scale):
    ms = jnp.mean(x * x, axis=-1, keepdims=True)
    y = x * lax.rsqrt(ms + NORM_EPS)
    return (y * g) * (1.0 + scale) + shift


def _ada_kernel(c_ref, w_ref, b_ref, o_ref):
    c = c_ref[...]
    ca = c * jax.nn.sigmoid(c)
    o_ref[0] = jnp.dot(ca, w_ref[0], preferred_element_type=F32,
                       precision=lax.Precision.HIGHEST) + b_ref[0]


def _ada(c_pad, ada_w, ada_b):
    depth, d, d6 = ada_w.shape
    bp = c_pad.shape[0]
    nb = d6 // d
    return pl.pallas_call(
        _ada_kernel,
        grid=(depth, nb),
        in_specs=[
            pl.BlockSpec((bp, d), lambda i, j: (0, 0)),
            pl.BlockSpec((1, d, d), lambda i, j: (i, 0, j)),
            pl.BlockSpec((1, 1, d), lambda i, j: (i, 0, j)),
        ],
        out_specs=pl.BlockSpec((1, bp, d), lambda i, j: (i, 0, j)),
        out_shape=jax.ShapeDtypeStruct((depth, bp, d6), F32),
        compiler_params=_cparams(2),
        name="ada",
    )(c_pad, ada_w, ada_b.reshape(depth, 1, d6))


def _conv_layer_kernel(x_ref, mod_ref, tn_ref, cn_ref, win_ref, cw_ref, wout_ref, uf_ref, vf_ref,
                       x1_ref, h2_ref, ub_ref, vb_ref, ubuf, *, ts):
    ub_ref[...] = uf_ref[...].astype(BF16)
    vb_ref[...] = vf_ref[...].astype(BF16)

    d = D_MODEL
    x = x_ref[0]
    m = mod_ref[0]
    sh1, sc1, g1, sh2, sc2 = m[0:1], m[1:2], m[2:3], m[3:4], m[4:5]
    h = _modulate(x, tn_ref[...], sh1, sc1)
    bcv = jnp.dot(h.astype(BF16), win_ref[...], preferred_element_type=F32)
    b_gate = bcv[:, :d]
    u = bcv[:, d:2 * d] * bcv[:, 2 * d:]

    @pl.when(pl.program_id(1) == 0)
    def _():
        ubuf[0:SUBLANES, :] = jnp.zeros((SUBLANES, d), F32)

    ubuf[SUBLANES:SUBLANES + ts, :] = u
    u1 = ubuf[SUBLANES - 1:SUBLANES - 1 + ts, :]
    u2 = ubuf[SUBLANES - 2:SUBLANES - 2 + ts, :]
    cw = cw_ref[...]
    conv = cw[0:1] * u2 + cw[1:2] * u1 + cw[2:3] * u
    ubuf[0:SUBLANES, :] = u[ts - SUBLANES:, :]
    y = jnp.dot((b_gate * conv).astype(BF16), wout_ref[...], preferred_element_type=F32)
    x1 = x + g1 * y
    x1_ref[0] = x1
    h2_ref[0] = _modulate(x1, cn_ref[...], sh2, sc2)


def _conv_layer(x, mod, tok_norm, ch_norm, w_in, conv_w, w_out, u_tab, v_tab, layer):
    b, s, d = x.shape
    ts = min(SEQ_TILE, s)
    nj = s // ts
    kern = functools.partial(_conv_layer_kernel, ts=ts)
    const = lambda *shape: pl.BlockSpec(shape, lambda i, j: (0,) * len(shape))
    step = lambda i, j: i * nj + j
    u_in, u_out, u_shape = _table_cast_specs(u_tab, layer, b * nj, step)
    v_in, v_out, v_shape = _table_cast_specs(v_tab, layer, b * nj, step)
    return pl.pallas_call(
        kern,
        grid=(b, nj),
        in_specs=[
            pl.BlockSpec((1, ts, d), lambda i, j: (i, j, 0)),
            pl.BlockSpec((1, 6, d), lambda i, j: (i, 0, 0)),
            const(1, d), const(1, d), const(d, 3 * d), const(3, d), const(d, d), u_in, v_in,
        ],
        out_specs=[pl.BlockSpec((1, ts, d), lambda i, j: (i, j, 0))] * 2 + [u_out, v_out],
        out_shape=[jax.ShapeDtypeStruct((b, s, d), F32)] * 2 + [u_shape, v_shape],
        scratch_shapes=[pltpu.VMEM((ts + SUBLANES, d), F32)],
        compiler_params=_cparams(2),
        name="conv_layer",
    )(x, mod, tok_norm, ch_norm, w_in, conv_w, w_out, u_tab, v_tab)


def _mla_proj_kernel(x_ref, mod_ref, tn_ref, pos_ref, freq_ref, sgn_ref,
                     wq_ref, wkv_ref, wkr_ref, qag_ref, kvag_ref,
                     wqbn_ref, wqbr_ref, wkvb_ref, qgn_ref, qgr_ref, kgn_ref, kgr_ref,
                     q_ref, k_ref, v_ref, *, ts):
    x = x_ref[0]
    m = mod_ref[0]
    h = _modulate(x, tn_ref[...], m[0:1], m[1:2]).astype(BF16)
    q_lat = jnp.dot(h, wq_ref[...], preferred_element_type=F32)
    kv_lat = jnp.dot(h, wkv_ref[...], preferred_element_type=F32)
    k_rope = jnp.dot(h, wkr_ref[...], preferred_element_type=F32)

    def rms(v, g):
        return v * lax.rsqrt(jnp.mean(v * v, axis=-1, keepdims=True) + NORM_EPS) * g

    qn = rms(q_lat, qag_ref[...]).astype(BF16)
    kvn = rms(kv_lat, kvag_ref[...]).astype(BF16)
    q_nope = jnp.dot(qn, wqbn_ref[...], preferred_element_type=F32)
    q_rope = jnp.dot(qn, wqbr_ref[...], preferred_element_type=F32)
    kv = jnp.dot(kvn, wkvb_ref[...], preferred_element_type=F32)

    ang = pos_ref[0].astype(F32) * freq_ref[...]
    cos = jnp.cos(ang)
    sin_signed = jnp.sin(ang) * sgn_ref[...]
    lane = lax.broadcasted_iota(jnp.int32, (ts, LANES), 1)
    first_half = (lane % QK_ROPE) < QK_ROPE // 2
    block_of = [lane < QK_ROPE, lane >= QK_ROPE]

    def rope(v):
        rot = jnp.where(first_half, pltpu.roll(v, LANES - QK_ROPE // 2, axis=1),
                        pltpu.roll(v, QK_ROPE // 2, axis=1))
        return v * cos + rot * sin_signed

    scale = QK_HEAD ** -0.5 * math.log2(math.e)
    kr_g = rope(k_rope * kgr_ref[...])
    kr_ss = 0.5 * jnp.sum(k_rope * k_rope, axis=-1, keepdims=True)
    pair_raw, pair_g = None, None
    for hd in range(MLA_HEADS):
        own = block_of[hd % 2]
        if hd % 2 == 0:
            pair_raw = q_rope[:, (hd // 2) * LANES:(hd // 2 + 1) * LANES]
            pair_g = rope(pair_raw * qgr_ref[...])
        qn_h = q_nope[:, hd * QK_NOPE:(hd + 1) * QK_NOPE]
        qr_h = jnp.where(own, pair_raw, 0.0)
        ss = jnp.sum(qn_h * qn_h + qr_h * qr_h, axis=-1, keepdims=True)
        rinv = lax.rsqrt(ss * (1.0 / QK_HEAD) + NORM_EPS) * scale
        qa = qn_h * rinv * qgn_ref[...]
        qb = jnp.where(own, pair_g, 0.0) * rinv
        q_ref[0, hd] = jnp.concatenate([qa, qb], axis=1).astype(BF16)

        kn_h = kv[:, hd * 2 * QK_NOPE:hd * 2 * QK_NOPE + QK_NOPE]
        v_h = kv[:, hd * 2 * QK_NOPE + QK_NOPE:(hd + 1) * 2 * QK_NOPE]
        kss = jnp.sum(kn_h * kn_h, axis=-1, keepdims=True) + kr_ss
        krinv = lax.rsqrt(kss * (1.0 / QK_HEAD) + NORM_EPS)
        ka = kn_h * krinv * kgn_ref[...]
        kb = jnp.where(own, kr_g, 0.0) * krinv
        k_ref[0, hd] = jnp.concatenate([ka, kb], axis=1).astype(BF16)
        v_ref[0, hd] = v_h.astype(BF16)


def _mla_proj(x, mod, tok_norm, pos3, freq, sgn, wq, wkv, wkr, qag, kvag,
              wqbn, wqbr, wkvb, qgn, qgr, kgn, kgr):
    b, s, d = x.shape
    ts = min(SEQ_TILE, s)
    kern = functools.partial(_mla_proj_kernel, ts=ts)
    consts = [tok_norm]
    rest = [freq, sgn, wq, wkv, wkr, qag, kvag, wqbn, wqbr, wkvb, qgn, qgr, kgn, kgr]
    cspec = lambda a: pl.BlockSpec(a.shape, lambda i, j: (0,) * a.ndim)
    hq = pl.BlockSpec((1, MLA_HEADS, ts, QK_PAD), lambda i, j: (i, 0, j, 0))
    hv = pl.BlockSpec((1, MLA_HEADS, ts, V_HEAD), lambda i, j: (i, 0, j, 0))
    return pl.pallas_call(
        kern,
        grid=(b, s // ts),
        in_specs=[
            pl.BlockSpec((1, ts, d), lambda i, j: (i, j, 0)),
            pl.BlockSpec((1, 6, d), lambda i, j: (i, 0, 0)),
            cspec(tok_norm),
            pl.BlockSpec((1, ts, 1), lambda i, j: (i, j, 0)),
        ] + [cspec(a) for a in rest],
        out_specs=[hq, hq, hv],
        out_shape=[
            jax.ShapeDtypeStruct((b, MLA_HEADS, s, QK_PAD), BF16),
            jax.ShapeDtypeStruct((b, MLA_HEADS, s, QK_PAD), BF16),
            jax.ShapeDtypeStruct((b, MLA_HEADS, s, V_HEAD), BF16),
        ],
        compiler_params=_cparams(2),
        name="mla_proj",
    )(x, mod, tok_norm, pos3, *rest)


FLASH_HEADS = 4


def _flash_kernel(q_ref, k_ref, v_ref, o_ref, *, tq):
    i = pl.program_id(2)

    def block(j, carries, diagonal):
        start = pl.multiple_of(j * tq, tq)
        out = []
        for hh in range(FLASH_HEADS):
            m, l, acc = carries[hh]
            kj = k_ref[0, hh, pl.ds(start, tq), :]
            vj = v_ref[0, hh, pl.ds(start, tq), :]
            s = lax.dot_general(q_ref[0, hh], kj, _NT, preferred_element_type=F32)
            if diagonal:
                row = lax.broadcasted_iota(jnp.int32, (tq, tq), 0)
                col = lax.broadcasted_iota(jnp.int32, (tq, tq), 1)
                s = jnp.where(row >= col, s, NEG_INF)
            m_new = jnp.maximum(m, jnp.max(s, axis=-1, keepdims=True))
            p = jnp.exp2(s - m_new)
            alpha = jnp.exp2(m - m_new)
            l = alpha * l + jnp.sum(p, axis=-1, keepdims=True)
            acc = alpha * acc + jnp.dot(p.astype(BF16), vj, preferred_element_type=F32)
            out.append((m_new, l, acc))
        return tuple(out)

    init = tuple((jnp.full((tq, 1), NEG_INF, F32), jnp.zeros((tq, 1), F32),
                  jnp.zeros((tq, V_HEAD), F32)) for _ in range(FLASH_HEADS))
    carries = lax.fori_loop(0, i, lambda j, c: block(j, c, False), init)
    carries = block(i, carries, True)
    o_ref[0] = jnp.concatenate([(acc / l).astype(BF16) for _, l, acc in carries], axis=1)


def _flash(q, k, v):
    b, hds, s, _ = q.shape
    tq = min(SEQ_TILE, s)
    kern = functools.partial(_flash_kernel, tq=tq)
    fh = FLASH_HEADS
    return pl.pallas_call(
        kern,
        grid=(b, hds // fh, s // tq),
        in_specs=[
            pl.BlockSpec((1, fh, tq, QK_PAD), lambda bi, h, i: (bi, h, i, 0)),
            pl.BlockSpec((1, fh, s, QK_PAD), lambda bi, h, i: (bi, h, 0, 0)),
            pl.BlockSpec((1, fh, s, V_HEAD), lambda bi, h, i: (bi, h, 0, 0)),
        ],
        out_specs=pl.BlockSpec((1, tq, fh * V_HEAD), lambda bi, h, i: (bi, i, h)),
        out_shape=jax.ShapeDtypeStruct((b, s, hds * V_HEAD), BF16),
        compiler_params=_cparams(3),
        name="mla_flash",
    )(q, k, v)


def _mla_out_kernel(o_ref, x_ref, mod_ref, cn_ref, w_ref, x1_ref, h2_ref):
    m = mod_ref[0]
    y = jnp.dot(o_ref[0], w_ref[...], preferred_element_type=F32)
    x1 = x_ref[0] + m[2:3] * y
    x1_ref[0] = x1
    h2_ref[0] = _modulate(x1, cn_ref[...], m[3:4], m[4:5])


def _mla_out(o, x, mod, ch_norm, w_out):
    b, s, d = x.shape
    ts = min(SEQ_TILE, s)
    return pl.pallas_call(
        _mla_out_kernel,
        grid=(b, s // ts),
        in_specs=[
            pl.BlockSpec((1, ts, d), lambda i, j: (i, j, 0)),
            pl.BlockSpec((1, ts, d), lambda i, j: (i, j, 0)),
            pl.BlockSpec((1, 6, d), lambda i, j: (i, 0, 0)),
            pl.BlockSpec((1, d), lambda i, j: (0, 0)),
            pl.BlockSpec((d, d), lambda i, j: (0, 0)),
        ],
        out_specs=[pl.BlockSpec((1, ts, d), lambda i, j: (i, j, 0))] * 2,
        out_shape=[jax.ShapeDtypeStruct((b, s, d), F32)] * 2,
        compiler_params=_cparams(2),
        name="mla_out",
    )(o, x, mod, ch_norm, w_out)


def _sort16_network():
    def merge(lo, hi, r):
        step = r * 2
        if step < hi - lo:
            yield from merge(lo, hi, step)
            yield from merge(lo + r, hi, step)
            yield from ((i, i + r) for i in range(lo + r, hi - r, step))
        else:
            yield (lo, lo + r)

    def sort(lo, hi):
        if hi - lo >= 1:
            mid = lo + (hi - lo) // 2
            yield from sort(lo, mid)
            yield from sort(mid + 1, hi)
            yield from merge(lo, hi, 1)

    return tuple(sort(0, PEER_TOPK - 1))


_SORT16 = _sort16_network()


def _top16_keys(s, key, stack):
    n, l = s.shape
    depth = n // SUBLANES
    val = [s[SUBLANES * v:SUBLANES * (v + 1), :] for v in range(depth)]
    pay = [key[SUBLANES * v:SUBLANES * (v + 1), :] for v in range(depth)]
    for a, b in _SORT16:
        swap = (val[b] > val[a]) | ((val[b] == val[a]) & (pay[b] < pay[a]))
        val[a], val[b] = jnp.where(swap, val[b], val[a]), jnp.where(swap, val[a], val[b])
        pay[a], pay[b] = jnp.where(swap, pay[b], pay[a]), jnp.where(swap, pay[a], pay[b])
    vals, idxs = [], []
    for r in range(PEER_TOPK):
        mx = jnp.max(val[0], axis=0, keepdims=True)
        idx = jnp.min(jnp.where(val[0] == mx, pay[0], float(n)), axis=0, keepdims=True)
        hit = pay[0] == idx
        vals.append(mx)
        idxs.append(idx)
        for p in range(PEER_TOPK - 1 - r):
            val[p] = jnp.where(hit, val[p + 1], val[p])
            pay[p] = jnp.where(hit, pay[p + 1], pay[p])
    return stack(vals), stack(idxs)


def _top16_pairs(s0, i0, s1, i1, stack):
    l = s0.shape[1]
    sub = lax.broadcasted_iota(jnp.int32, (SUBLANES, l), 0)
    subf = sub.astype(F32)
    lists = []
    for p in range(PEER_TOPK):
        plane = s0[0:SUBLANES] + s1[p:p + 1]
        n_valid = PEER_TOPK // (p + 1)
        if n_valid < SUBLANES:
            plane = jnp.where(sub < n_valid, plane, NEG_INF)
        lists.append(plane)
    tail = s0[SUBLANES:] + s1[0:1]
    pos_tail = (subf + float(SUBLANES)) * float(PEER_TOPK)
    pops = jnp.zeros((SUBLANES, l), F32)
    no_pos = float(PEER_TOPK * PEER_TOPK)
    vals, poss = [], []
    for r in range(PEER_TOPK):
        pos_head = subf * float(PEER_TOPK) + pops
        mx = jnp.max(jnp.maximum(lists[0], tail), axis=0, keepdims=True)
        cand = jnp.minimum(jnp.where(lists[0] == mx, pos_head, no_pos),
                           jnp.where(tail == mx, pos_tail, no_pos))
        pos = jnp.min(cand, axis=0, keepdims=True)
        hit = pos_head == pos
        vals.append(mx)
        poss.append(pos)
        pops = pops + jnp.where(hit, 1.0, 0.0)
        for p in range(PEER_TOPK - 1 - r):
            lists[p] = jnp.where(hit, lists[p + 1], lists[p])
        tail = jnp.where(pos_tail == pos, NEG_INF, tail)
    vals, poss = stack(vals), stack(poss)
    pi = jnp.floor(poss * (1.0 / PEER_TOPK))
    pj = poss - pi * float(PEER_TOPK)
    ia = jnp.zeros((PEER_TOPK, l), F32)
    ib = jnp.zeros((PEER_TOPK, l), F32)
    for k in range(PEER_TOPK):
        ia = jnp.where(pi == float(k), i0[k:k + 1], ia)
        ib = jnp.where(pj == float(k), i1[k:k + 1], ib)
    return vals, ia, ib


W_PITCH = 136


def _peer_route_kernel(h_ref, wq_ref, sk_ref, w_ref, hb_ref,
                       rows_s, iat_s, ibt_s, gt_s, ia_s, ib_s, ghi_s, glo_s, wbuf, *, tt):

    @pl.when(pl.program_id(0) == 0)
    def _():
        for ref in (ia_s, ib_s, ghi_s, glo_s):
            ref[...] = jnp.zeros_like(ref)

    hb_ref[...] = h_ref[...].astype(BF16)

    def one_head(hd, key):
        planes = iter(range(rows_s.shape[1]))

        def stack(rows):
            plane = rows_s.at[hd, next(planes)]
            for r, row in enumerate(rows):
                plane[r:r + 1, :] = row
            return plane[...]

        cols = slice(hd * 2 * PEER_HALF, (hd + 1) * 2 * PEER_HALF)
        q = jnp.dot(hb_ref[...], wq_ref[:, cols], preferred_element_type=F32).astype(BF16)
        tops = []
        for p in range(2):
            s = lax.dot_general(sk_ref[hd, p], q[:, p * PEER_HALF:(p + 1) * PEER_HALF], _NT,
                                preferred_element_type=F32)
            tops.append(_top16_keys(s, key, stack))
        (s0, i0), (s1, i1) = tops
        best, ia, ib = _top16_pairs(s0, i0, s1, i1, stack)
        e = jnp.exp(best - best[0:1])
        g = e / jnp.sum(e, axis=0, keepdims=True)
        rows = pl.ds(pl.multiple_of(hd * PEER_TOPK, PEER_TOPK), PEER_TOPK)
        iat_s[rows, :] = ia
        ibt_s[rows, :] = ib
        gt_s[rows, :] = g

    onehot_key = lax.broadcasted_iota(jnp.int32, (PEER_KEYS, LANES), 0).astype(F32).astype(BF16)
    zero = jnp.zeros((PEER_KEYS, LANES), BF16)
    one = jnp.ones((PEER_KEYS, LANES), BF16)
    pack_rows = 2 * SUBLANES

    def row_plane(ref, t):
        tile = jnp.broadcast_to(ref[t:t + 1, :], (pack_rows, LANES)).astype(BF16)
        return jnp.concatenate([tile] * (PEER_KEYS // pack_rows), axis=0)

    def scatter_token(t):
        hit_a = row_plane(ia_s, t) == onehot_key
        hit_b = row_plane(ib_s, t) == onehot_key
        a_hi = jnp.where(hit_a, row_plane(ghi_s, t), zero)
        a_lo = jnp.where(hit_a, row_plane(glo_s, t), zero)
        b_one = jnp.where(hit_b, one, zero)
        lhs = jnp.concatenate([a_hi, a_lo], axis=1)
        rhs = jnp.concatenate([b_one, b_one], axis=1)
        w = lax.dot_general(lhs, rhs, _NT, preferred_element_type=F32)
        wbuf[pl.ds(t, PEER_KEYS, stride=W_PITCH), :] = w

    key = lax.broadcasted_iota(jnp.int32, (PEER_KEYS, tt), 0).astype(F32)
    per_head = tt // PEER_HEADS
    for hd in range(PEER_HEADS):
        one_head(hd, key)
        for t in range(hd * per_head, (hd + 1) * per_head):
            scatter_token(t)
    for ia in range(PEER_KEYS):
        w_ref[ia] = wbuf[ia * W_PITCH:ia * W_PITCH + tt, :].astype(BF16)

    ia_s[...] = iat_s[...].T
    ib_s[...] = ibt_s[...].T
    g = gt_s[...].T
    g_hi = g.astype(BF16).astype(F32)
    ghi_s[...] = g_hi
    glo_s[...] = g - g_hi


def _peer_route(h2, wq, sk):
    t, d = h2.shape
    tt = ROUTE_TILE
    n = t // tt
    kern = functools.partial(_peer_route_kernel, tt=tt)
    cspec = lambda a: pl.BlockSpec(a.shape, lambda i: (0,) * a.ndim)
    slab = pltpu.VMEM((PEER_HEADS * PEER_TOPK, tt), F32)
    tok = pltpu.VMEM((tt, PEER_HEADS * PEER_TOPK), F32)
    cur = lambda i: (jnp.minimum(i, n - 1), 0)
    return pl.pallas_call(
        kern,
        grid=(n + 1,),
        in_specs=[pl.BlockSpec((tt, d), cur),
                  cspec(wq), cspec(sk)],
        out_specs=[pl.BlockSpec((PEER_KEYS, tt, PEER_KEYS),
                                lambda i: (0, jnp.maximum(i - 1, 0), 0)),
                   pl.BlockSpec((tt, d), cur)],
        out_shape=[jax.ShapeDtypeStruct((PEER_KEYS, t, PEER_KEYS), BF16),
                   jax.ShapeDtypeStruct((t, d), BF16)],
        scratch_shapes=[pltpu.VMEM((PEER_HEADS, 6, PEER_TOPK, tt), F32), slab, slab, slab, tok, tok, tok, tok,
                        pltpu.VMEM((PEER_KEYS * W_PITCH, PEER_KEYS), F32)],
        compiler_params=_cparams(1),
        name="peer_route",
    )(h2, wq, sk)


def _gelu(z):
    return 0.5 * z * (1.0 + lax.erf(z * (1.0 / math.sqrt(2.0))))


def _peer_dense_kernel(h_ref, w_ref, u_ref, v_ref, x_ref, mod_ref, *rest, n_ia, cast_next):
    if cast_next:
        uf_ref, vf_ref, o_ref, ub_ref, vb_ref, acc_ref = rest
        ub_ref[...] = uf_ref[...].astype(BF16)
        vb_ref[...] = vf_ref[...].astype(BF16)
    else:
        o_ref, acc_ref = rest
    j = pl.program_id(1)

    @pl.when(j == 0)
    def _():
        acc_ref[...] = jnp.zeros_like(acc_ref)

    z = lax.dot_general(h_ref[...], u_ref[...], _NT, preferred_element_type=F32)
    w = jnp.concatenate([w_ref[i] for i in range(n_ia)], axis=1)
    a = (_gelu(z) * w.astype(F32)).astype(BF16)
    acc_ref[...] += jnp.dot(a, v_ref[...], preferred_element_type=F32)

    @pl.when(j == pl.num_programs(1) - 1)
    def _():
        o_ref[...] = x_ref[...] + mod_ref[0][5:6] * acc_ref[...]


def _peer_dense(h_bf, w3, u_bf, v_bf, x, mod, seq, next_tables):
    t, d = h_bf.shape
    tt = min(DENSE_TILE, seq)
    n_ia = DENSE_SLABS
    e_blk = n_ia * PEER_KEYS
    ni, nj = t // tt, PEER_EXPERTS // e_blk
    cast_next = next_tables is not None
    kern = functools.partial(_peer_dense_kernel, n_ia=n_ia, cast_next=cast_next)
    in_specs = [
        pl.BlockSpec((tt, d), lambda i, j: (i, 0)),
        pl.BlockSpec((n_ia, tt, PEER_KEYS), lambda i, j: (j, i, 0)),
        pl.BlockSpec((e_blk, d), lambda i, j: (j, 0)),
        pl.BlockSpec((e_blk, d), lambda i, j: (j, 0)),
        pl.BlockSpec((tt, d), lambda i, j: (i, 0)),
        pl.BlockSpec((1, 6, d), lambda i, j: ((i * tt) // seq, 0, 0)),
    ]
    out_specs = [pl.BlockSpec((tt, d), lambda i, j: (i, 0))]
    out_shape = [jax.ShapeDtypeStruct((t, d), F32)]
    args = [h_bf, w3, u_bf, v_bf, x, mod]
    if cast_next:
        u_tab, v_tab, layer = next_tables
        step = lambda i, j: i * nj + j
        for tab in (u_tab, v_tab):
            t_in, t_out, t_shape = _table_cast_specs(tab, layer, ni * nj, step)
            in_specs.append(t_in)
            out_specs.append(t_out)
            out_shape.append(t_shape)
            args.append(tab)
    return pl.pallas_call(
        kern,
        grid=(ni, nj),
        in_specs=in_specs,
        out_specs=out_specs,
        out_shape=out_shape,
        scratch_shapes=[pltpu.VMEM((tt, d), F32)],
        compiler_params=_cparams(2),
        name="peer_dense",
    )(*args)


def _peer(h2, x, mod, w_q, subkeys, u_bf, v_bf, next_tables):
    b, s, d = x.shape
    h2f = h2.reshape(b * s, d)
    w3, h_bf = _peer_route(h2f, w_q.astype(BF16), subkeys.astype(BF16))
    out, *tables = _peer_dense(h_bf, w3, u_bf, v_bf, x.reshape(b * s, d), mod, s, next_tables)
    return out.reshape(b, s, d), tables


def kernel(x, c, positions, tok_norm, ch_norm, ada_w, ada_b, conv_in_w, conv_w, conv_out_w,
           mla_in_w, q_a_norm, kv_a_norm, q_b_w, kv_b_w, q_norm, k_norm, mla_out_w,
           peer_q_w, peer_subkeys, peer_u, peer_v):
    b, s, d = x.shape
    depth = ada_w.shape[0]
    bp = -(-b // SUBLANES) * SUBLANES
    c_pad = jnp.zeros((bp, d), F32).at[:b].set(c)
    mod_all = _ada(c_pad, ada_w, ada_b)[:, :b].reshape(depth, b, 6, d)

    u_bf = v_bf = None
    row = lambda a: a.reshape(1, -1)
    for i in range(depth):
        mod = mod_all[i]
        j = i // 2
        if i % 2 == 0:
            x, h2, u_now, v_now = _conv_layer(x, mod, row(tok_norm[i]), row(ch_norm[i]),
                                              conv_in_w[j].astype(BF16), conv_w[j],
                                              conv_out_w[j].astype(BF16), peer_u, peer_v, i)
            if u_bf is None:
                u_bf, v_bf = u_now, v_now
        else:
            w_in = mla_in_w[j].astype(BF16)
            wq = w_in[:, :Q_LORA]
            wkv = w_in[:, Q_LORA:Q_LORA + KV_LORA]
            wkr = jnp.tile(w_in[:, Q_LORA + KV_LORA:], (1, LANES // QK_ROPE))
            wqb = q_b_w[j].astype(BF16).reshape(Q_LORA, MLA_HEADS, QK_HEAD)
            wqbn = wqb[:, :, :QK_NOPE].reshape(Q_LORA, MLA_HEADS * QK_NOPE)
            wqbr = wqb[:, :, QK_NOPE:].reshape(Q_LORA, MLA_HEADS * QK_ROPE)
            inv_freq = ROPE_THETA ** (-jnp.arange(0, QK_ROPE, 2, dtype=F32) / QK_ROPE)
            blocks = LANES // QK_ROPE
            half = QK_ROPE // 2
            freq = jnp.tile(inv_freq, 2 * blocks).reshape(1, LANES)
            sgn = jnp.tile(jnp.concatenate([-jnp.ones((half,), F32), jnp.ones((half,), F32)]),
                           blocks).reshape(1, LANES)
            q, k, v = _mla_proj(
                x, mod, row(tok_norm[i]), positions.reshape(b, s, 1), freq, sgn,
                wq, wkv, wkr, row(q_a_norm[j]), row(kv_a_norm[j]),
                wqbn, wqbr, kv_b_w[j].astype(BF16),
                row(q_norm[j][:QK_NOPE]), row(jnp.tile(q_norm[j][QK_NOPE:], blocks)),
                row(k_norm[j][:QK_NOPE]), row(jnp.tile(k_norm[j][QK_NOPE:], blocks)))
            o = _flash(q, k, v)
            x, h2 = _mla_out(o, x, mod, row(ch_norm[i]), mla_out_w[j].astype(BF16))
        next_tables = (peer_u, peer_v, i + 1) if i + 1 < depth else None
        x, tables = _peer(h2, x, mod, peer_q_w[i], peer_subkeys[i], u_bf, v_bf, next_tables)
        if tables:
            u_bf, v_bf = tables
    return x
```

```python
import functools
import math

import jax
import jax.numpy as jnp
from jax import lax
from jax.experimental import pallas as pl
from jax.experimental.pallas import tpu as pltpu

F32 = jnp.float32
BF16 = jnp.bfloat16

D_MODEL = 1024
MLA_HEADS = 8
Q_LORA = 384
KV_LORA = 256
QK_NOPE = 128
QK_ROPE = 64
QK_HEAD = QK_NOPE + QK_ROPE
QK_PAD = 256
V_HEAD = 128
ROPE_THETA = 10000.0
PEER_HEADS = 8
PEER_KEYS = 128
PEER_EXPERTS = PEER_KEYS * PEER_KEYS
PEER_HALF = 128
PEER_TOPK = 16
NORM_EPS = 1e-6

LANES = 128
SUBLANES = 8
VMEM_LIMIT = 56 * 1024 * 1024

SEQ_TILE = 512
ROUTE_TILE = 128
DENSE_TILE = 1024
DENSE_SLABS = SUBLANES

NEG_INF = float("-inf")
_NT = (((1,), (1,)), ((), ()))


def _cparams(n_axes):
    return pltpu.CompilerParams(
        dimension_semantics=("arbitrary",) * n_axes, vmem_limit_bytes=VMEM_LIMIT)


def _table_cast_specs(table, layer, n_steps, step_index):
    _, n, d = table.shape
    rows = n // n_steps
    assert rows * n_steps == n and rows % (2 * SUBLANES) == 0
    in_spec = pl.BlockSpec((None, rows, d), lambda *g: (layer, step_index(*g), 0))
    out_spec = pl.BlockSpec((rows, d), lambda *g: (step_index(*g), 0))
    return in_spec, out_spec, jax.ShapeDtypeStruct((n, d), BF16)


def _modulate(x, g, shift, scale):
    ms = jnp.mean(x * x, axis=-1, keepdims=True)
    y = x * lax.rsqrt(ms + NORM_EPS)
    return (y * g) * (1.0 + scale) + shift


def _ada_kernel(c_ref, w_ref, b_ref, o_ref):
    c = c_ref[...]
    ca = c * jax.nn.sigmoid(c)
    o_ref[0] = jnp.dot(ca, w_ref[0], preferred_element_type=F32,
                       precision=lax.Precision.HIGHEST) + b_ref[0]


def _ada(c_pad, ada_w, ada_b):
    depth, d, d6 = ada_w.shape
    bp = c_pad.shape[0]
    nb = d6 // d
    return pl.pallas_call(
        _ada_kernel,
        grid=(depth, nb),
        in_specs=[
            pl.BlockSpec((bp, d), lambda i, j: (0, 0)),
            pl.BlockSpec((1, d, d), lambda i, j: (i, 0, j)),
            pl.BlockSpec((1, 1, d), lambda i, j: (i, 0, j)),
        ],
        out_specs=pl.BlockSpec((1, bp, d), lambda i, j: (i, 0, j)),
        out_shape=jax.ShapeDtypeStruct((depth, bp, d6), F32),
        compiler_params=_cparams(2),
        name="ada",
    )(c_pad, ada_w, ada_b.reshape(depth, 1, d6))


def _conv_layer_kernel(x_ref, mod_ref, tn_ref, cn_ref, win_ref, cw_ref, wout_ref, uf_ref, vf_ref,
                       x1_ref, h2_ref, ub_ref, vb_ref, ubuf, *, ts):
    ub_ref[...] = uf_ref[...].astype(BF16)
    vb_ref[...] = vf_ref[...].astype(BF16)

    d = D_MODEL
    x = x_ref[0]
    m = mod_ref[0]
    sh1, sc1, g1, sh2, sc2 = m[0:1], m[1:2], m[2:3], m[3:4], m[4:5]
    h = _modulate(x, tn_ref[...], sh1, sc1)
    bcv = jnp.dot(h.astype(BF16), win_ref[...], preferred_element_type=F32)
    b_gate = bcv[:, :d]
    u = bcv[:, d:2 * d] * bcv[:, 2 * d:]

    @pl.when(pl.program_id(1) == 0)
    def _():
        ubuf[0:SUBLANES, :] = jnp.zeros((SUBLANES, d), F32)

    ubuf[SUBLANES:SUBLANES + ts, :] = u
    u1 = ubuf[SUBLANES - 1:SUBLANES - 1 + ts, :]
    u2 = ubuf[SUBLANES - 2:SUBLANES - 2 + ts, :]
    cw = cw_ref[...]
    conv = cw[0:1] * u2 + cw[1:2] * u1 + cw[2:3] * u
    ubuf[0:SUBLANES, :] = u[ts - SUBLANES:, :]
    y = jnp.dot((b_gate * conv).astype(BF16), wout_ref[...], preferred_element_type=F32)
    x1 = x + g1 * y
    x1_ref[0] = x1
    h2_ref[0] = _modulate(x1, cn_ref[...], sh2, sc2).astype(BF16)


def _conv_layer(x, mod, tok_norm, ch_norm, w_in, conv_w, w_out, u_tab, v_tab, layer):
    b, s, d = x.shape
    ts = min(SEQ_TILE, s)
    nj = s // ts
    kern = functools.partial(_conv_layer_kernel, ts=ts)
    const = lambda *shape: pl.BlockSpec(shape, lambda i, j: (0,) * len(shape))
    step = lambda i, j: i * nj + j
    u_in, u_out, u_shape = _table_cast_specs(u_tab, layer, b * nj, step)
    v_in, v_out, v_shape = _table_cast_specs(v_tab, layer, b * nj, step)
    return pl.pallas_call(
        kern,
        grid=(b, nj),
        in_specs=[
            pl.BlockSpec((1, ts, d), lambda i, j: (i, j, 0)),
            pl.BlockSpec((1, 6, d), lambda i, j: (i, 0, 0)),
            const(1, d), const(1, d), const(d, 3 * d), const(3, d), const(d, d), u_in, v_in,
        ],
        out_specs=[pl.BlockSpec((1, ts, d), lambda i, j: (i, j, 0))] * 2 + [u_out, v_out],
        out_shape=[jax.ShapeDtypeStruct((b, s, d), F32), jax.ShapeDtypeStruct((b, s, d), BF16),
                   u_shape, v_shape],
        scratch_shapes=[pltpu.VMEM((ts + SUBLANES, d), F32)],
        compiler_params=_cparams(2),
        name="conv_layer",
    )(x, mod, tok_norm, ch_norm, w_in, conv_w, w_out, u_tab, v_tab)


def _mla_proj_kernel(x_ref, mod_ref, tn_ref, pos_ref, freq_ref, sgn_ref,
                     wq_ref, wkv_ref, wkr_ref, qag_ref, kvag_ref,
                     wqbn_ref, wqbr_ref, wkvb_ref, qgn_ref, qgr_ref, kgn_ref, kgr_ref,
                     q_ref, k_ref, v_ref, *, ts):
    x = x_ref[0]
    m = mod_ref[0]
    h = _modulate(x, tn_ref[...], m[0:1], m[1:2]).astype(BF16)
    q_lat = jnp.dot(h, wq_ref[...], preferred_element_type=F32)
    kv_lat = jnp.dot(h, wkv_ref[...], preferred_element_type=F32)
    k_rope = jnp.dot(h, wkr_ref[...], preferred_element_type=F32)

    def rms(v, g):
        return v * lax.rsqrt(jnp.mean(v * v, axis=-1, keepdims=True) + NORM_EPS) * g

    qn = rms(q_lat, qag_ref[...]).astype(BF16)
    kvn = rms(kv_lat, kvag_ref[...]).astype(BF16)
    q_nope = jnp.dot(qn, wqbn_ref[...], preferred_element_type=F32)
    q_rope = jnp.dot(qn, wqbr_ref[...], preferred_element_type=F32)
    kv = jnp.dot(kvn, wkvb_ref[...], preferred_element_type=F32)

    ang = pos_ref[0].astype(F32) * freq_ref[...]
    cos = jnp.cos(ang)
    sin_signed = jnp.sin(ang) * sgn_ref[...]
    lane = lax.broadcasted_iota(jnp.int32, (ts, LANES), 1)
    first_half = (lane % QK_ROPE) < QK_ROPE // 2
    block_of = [lane < QK_ROPE, lane >= QK_ROPE]

    def rope(v):
        rot = jnp.where(first_half, pltpu.roll(v, LANES - QK_ROPE // 2, axis=1),
                        pltpu.roll(v, QK_ROPE // 2, axis=1))
        return v * cos + rot * sin_signed

    scale = QK_HEAD ** -0.5 * math.log2(math.e)
    kr_g = rope(k_rope * kgr_ref[...])
    kr_ss = 0.5 * jnp.sum(k_rope * k_rope, axis=-1, keepdims=True)
    pair_raw, pair_g = None, None
    for hd in range(MLA_HEADS):
        own = block_of[hd % 2]
        if hd % 2 == 0:
            pair_raw = q_rope[:, (hd // 2) * LANES:(hd // 2 + 1) * LANES]
            pair_g = rope(pair_raw * qgr_ref[...])
        qn_h = q_nope[:, hd * QK_NOPE:(hd + 1) * QK_NOPE]
        qr_h = jnp.where(own, pair_raw, 0.0)
        ss = jnp.sum(qn_h * qn_h + qr_h * qr_h, axis=-1, keepdims=True)
        rinv = lax.rsqrt(ss * (1.0 / QK_HEAD) + NORM_EPS) * scale
        qa = qn_h * rinv * qgn_ref[...]
        qb = jnp.where(own, pair_g, 0.0) * rinv
        q_ref[0, hd] = jnp.concatenate([qa, qb], axis=1).astype(BF16)

        kn_h = kv[:, hd * 2 * QK_NOPE:hd * 2 * QK_NOPE + QK_NOPE]
        v_h = kv[:, hd * 2 * QK_NOPE + QK_NOPE:(hd + 1) * 2 * QK_NOPE]
        kss = jnp.sum(kn_h * kn_h, axis=-1, keepdims=True) + kr_ss
        krinv = lax.rsqrt(kss * (1.0 / QK_HEAD) + NORM_EPS)
        ka = kn_h * krinv * kgn_ref[...]
        kb = jnp.where(own, kr_g, 0.0) * krinv
        k_ref[0, hd] = jnp.concatenate([ka, kb], axis=1).astype(BF16)
        v_ref[0, hd] = v_h.astype(BF16)


def _mla_proj(x, mod, tok_norm, pos3, freq, sgn, wq, wkv, wkr, qag, kvag,
              wqbn, wqbr, wkvb, qgn, qgr, kgn, kgr):
    b, s, d = x.shape
    ts = min(SEQ_TILE, s)
    kern = functools.partial(_mla_proj_kernel, ts=ts)
    consts = [tok_norm]
    rest = [freq, sgn, wq, wkv, wkr, qag, kvag, wqbn, wqbr, wkvb, qgn, qgr, kgn, kgr]
    cspec = lambda a: pl.BlockSpec(a.shape, lambda i, j: (0,) * a.ndim)
    hq = pl.BlockSpec((1, MLA_HEADS, ts, QK_PAD), lambda i, j: (i, 0, j, 0))
    hv = pl.BlockSpec((1, MLA_HEADS, ts, V_HEAD), lambda i, j: (i, 0, j, 0))
    return pl.pallas_call(
        kern,
        grid=(b, s // ts),
        in_specs=[
            pl.BlockSpec((1, ts, d), lambda i, j: (i, j, 0)),
            pl.BlockSpec((1, 6, d), lambda i, j: (i, 0, 0)),
            cspec(tok_norm),
            pl.BlockSpec((1, ts, 1), lambda i, j: (i, j, 0)),
        ] + [cspec(a) for a in rest],
        out_specs=[hq, hq, hv],
        out_shape=[
            jax.ShapeDtypeStruct((b, MLA_HEADS, s, QK_PAD), BF16),
            jax.ShapeDtypeStruct((b, MLA_HEADS, s, QK_PAD), BF16),
            jax.ShapeDtypeStruct((b, MLA_HEADS, s, V_HEAD), BF16),
        ],
        compiler_params=_cparams(2),
        name="mla_proj",
    )(x, mod, tok_norm, pos3, *rest)


FLASH_HEADS = 4


def _flash_kernel(q_ref, k_ref, v_ref, o_ref, *, tq):
    i = pl.program_id(2)

    def block(j, carries, diagonal):
        start = pl.multiple_of(j * tq, tq)
        out = []
        for hh in range(FLASH_HEADS):
            m, l, acc = carries[hh]
            kj = k_ref[0, hh, pl.ds(start, tq), :]
            vj = v_ref[0, hh, pl.ds(start, tq), :]
            s = lax.dot_general(q_ref[0, hh], kj, _NT, preferred_element_type=F32)
            if diagonal:
                row = lax.broadcasted_iota(jnp.int32, (tq, tq), 0)
                col = lax.broadcasted_iota(jnp.int32, (tq, tq), 1)
                s = jnp.where(row >= col, s, NEG_INF)
            m_new = jnp.maximum(m, jnp.max(s, axis=-1, keepdims=True))
            p = jnp.exp2(s - m_new)
            alpha = jnp.exp2(m - m_new)
            l = alpha * l + jnp.sum(p, axis=-1, keepdims=True)
            acc = alpha * acc + jnp.dot(p.astype(BF16), vj, preferred_element_type=F32)
            out.append((m_new, l, acc))
        return tuple(out)

    init = tuple((jnp.full((tq, 1), NEG_INF, F32), jnp.zeros((tq, 1), F32),
                  jnp.zeros((tq, V_HEAD), F32)) for _ in range(FLASH_HEADS))
    carries = lax.fori_loop(0, i, lambda j, c: block(j, c, False), init)
    carries = block(i, carries, True)
    o_ref[0] = jnp.concatenate([(acc / l).astype(BF16) for _, l, acc in carries], axis=1)


def _flash(q, k, v):
    b, hds, s, _ = q.shape
    tq = min(SEQ_TILE, s)
    kern = functools.partial(_flash_kernel, tq=tq)
    fh = FLASH_HEADS
    return pl.pallas_call(
        kern,
        grid=(b, hds // fh, s // tq),
        in_specs=[
            pl.BlockSpec((1, fh, tq, QK_PAD), lambda bi, h, i: (bi, h, i, 0)),
            pl.BlockSpec((1, fh, s, QK_PAD), lambda bi, h, i: (bi, h, 0, 0)),
            pl.BlockSpec((1, fh, s, V_HEAD), lambda bi, h, i: (bi, h, 0, 0)),
        ],
        out_specs=pl.BlockSpec((1, tq, fh * V_HEAD), lambda bi, h, i: (bi, i, h)),
        out_shape=jax.ShapeDtypeStruct((b, s, hds * V_HEAD), BF16),
        compiler_params=_cparams(3),
        name="mla_flash",
    )(q, k, v)


def _mla_out_kernel(o_ref, x_ref, mod_ref, cn_ref, w_ref, x1_ref, h2_ref):
    m = mod_ref[0]
    y = jnp.dot(o_ref[0], w_ref[...], preferred_element_type=F32)
    x1 = x_ref[0] + m[2:3] * y
    x1_ref[0] = x1
    h2_ref[0] = _modulate(x1, cn_ref[...], m[3:4], m[4:5]).astype(BF16)


def _mla_out(o, x, mod, ch_norm, w_out):
    b, s, d = x.shape
    ts = min(SEQ_TILE, s)
    return pl.pallas_call(
        _mla_out_kernel,
        grid=(b, s // ts),
        in_specs=[
            pl.BlockSpec((1, ts, d), lambda i, j: (i, j, 0)),
            pl.BlockSpec((1, ts, d), lambda i, j: (i, j, 0)),
            pl.BlockSpec((1, 6, d), lambda i, j: (i, 0, 0)),
            pl.BlockSpec((1, d), lambda i, j: (0, 0)),
            pl.BlockSpec((d, d), lambda i, j: (0, 0)),
        ],
        out_specs=[pl.BlockSpec((1, ts, d), lambda i, j: (i, j, 0))] * 2,
        out_shape=[jax.ShapeDtypeStruct((b, s, d), F32), jax.ShapeDtypeStruct((b, s, d), BF16)],
        compiler_params=_cparams(2),
        name="mla_out",
    )(o, x, mod, ch_norm, w_out)


def _sort16_network():
    def merge(lo, hi, r):
        step = r * 2
        if step < hi - lo:
            yield from merge(lo, hi, step)
            yield from merge(lo + r, hi, step)
            yield from ((i, i + r) for i in range(lo + r, hi - r, step))
        else:
            yield (lo, lo + r)

    def sort(lo, hi):
        if hi - lo >= 1:
            mid = lo + (hi - lo) // 2
            yield from sort(lo, mid)
            yield from sort(mid + 1, hi)
            yield from merge(lo, hi, 1)

    return tuple(sort(0, PEER_TOPK - 1))


_SORT16 = _sort16_network()


def _top16_keys(s, key, stack):
    n, l = s.shape
    depth = n // SUBLANES
    val = [s[SUBLANES * v:SUBLANES * (v + 1), :] for v in range(depth)]
    pay = [key[SUBLANES * v:SUBLANES * (v + 1), :] for v in range(depth)]
    for a, b in _SORT16:
        swap = (val[b] > val[a]) | ((val[b] == val[a]) & (pay[b] < pay[a]))
        val[a], val[b] = jnp.where(swap, val[b], val[a]), jnp.where(swap, val[a], val[b])
        pay[a], pay[b] = jnp.where(swap, pay[b], pay[a]), jnp.where(swap, pay[a], pay[b])
    vals, idxs = [], []
    for r in range(PEER_TOPK):
        mx = jnp.max(val[0], axis=0, keepdims=True)
        idx = jnp.min(jnp.where(val[0] == mx, pay[0], float(n)), axis=0, keepdims=True)
        hit = pay[0] == idx
        vals.append(mx)
        idxs.append(idx)
        for p in range(PEER_TOPK - 1 - r):
            val[p] = jnp.where(hit, val[p + 1], val[p])
            pay[p] = jnp.where(hit, pay[p + 1], pay[p])
    return stack(vals), stack(idxs)


def _top16_pairs(s0, i0, s1, i1, stack):
    l = s0.shape[1]
    sub = lax.broadcasted_iota(jnp.int32, (SUBLANES, l), 0)
    subf = sub.astype(F32)
    lists = []
    for p in range(PEER_TOPK):
        plane = s0[0:SUBLANES] + s1[p:p + 1]
        n_valid = PEER_TOPK // (p + 1)
        if n_valid < SUBLANES:
            plane = jnp.where(sub < n_valid, plane, NEG_INF)
        lists.append(plane)
    tail = s0[SUBLANES:] + s1[0:1]
    pos_tail = (subf + float(SUBLANES)) * float(PEER_TOPK)
    pops = jnp.zeros((SUBLANES, l), F32)
    no_pos = float(PEER_TOPK * PEER_TOPK)
    vals, poss = [], []
    for r in range(PEER_TOPK):
        pos_head = subf * float(PEER_TOPK) + pops
        mx = jnp.max(jnp.maximum(lists[0], tail), axis=0, keepdims=True)
        cand = jnp.minimum(jnp.where(lists[0] == mx, pos_head, no_pos),
                           jnp.where(tail == mx, pos_tail, no_pos))
        pos = jnp.min(cand, axis=0, keepdims=True)
        hit = pos_head == pos
        vals.append(mx)
        poss.append(pos)
        pops = pops + jnp.where(hit, 1.0, 0.0)
        for p in range(PEER_TOPK - 1 - r):
            lists[p] = jnp.where(hit, lists[p + 1], lists[p])
        tail = jnp.where(pos_tail == pos, NEG_INF, tail)
    vals, poss = stack(vals), stack(poss)
    pi = jnp.floor(poss * (1.0 / PEER_TOPK))
    pj = poss - pi * float(PEER_TOPK)
    ia = jnp.zeros((PEER_TOPK, l), F32)
    ib = jnp.zeros((PEER_TOPK, l), F32)
    for k in range(PEER_TOPK):
        ia = jnp.where(pi == float(k), i0[k:k + 1], ia)
        ib = jnp.where(pj == float(k), i1[k:k + 1], ib)
    return vals, ia, ib


W_PITCH = 136


def _peer_route_kernel(h_ref, wq_ref, sk_ref, w_ref,
                       rows_s, iat_s, ibt_s, gt_s, ia_s, ib_s, ghi_s, glo_s, wbuf, *, tt):

    @pl.when(pl.program_id(0) == 0)
    def _():
        for ref in (ia_s, ib_s, ghi_s, glo_s):
            ref[...] = jnp.zeros_like(ref)

    def one_head(hd, key):
        planes = iter(range(rows_s.shape[1]))

        def stack(rows):
            plane = rows_s.at[hd, next(planes)]
            for r, row in enumerate(rows):
                plane[r:r + 1, :] = row
            return plane[...]

        cols = slice(hd * 2 * PEER_HALF, (hd + 1) * 2 * PEER_HALF)
        q = jnp.dot(h_ref[...], wq_ref[:, cols], preferred_element_type=F32).astype(BF16)
        tops = []
        for p in range(2):
            s = lax.dot_general(sk_ref[hd, p], q[:, p * PEER_HALF:(p + 1) * PEER_HALF], _NT,
                                preferred_element_type=F32)
            tops.append(_top16_keys(s, key, stack))
        (s0, i0), (s1, i1) = tops
        best, ia, ib = _top16_pairs(s0, i0, s1, i1, stack)
        e = jnp.exp(best - best[0:1])
        g = e / jnp.sum(e, axis=0, keepdims=True)
        rows = pl.ds(pl.multiple_of(hd * PEER_TOPK, PEER_TOPK), PEER_TOPK)
        iat_s[rows, :] = ia
        ibt_s[rows, :] = ib
        gt_s[rows, :] = g

    onehot_key = lax.broadcasted_iota(jnp.int32, (PEER_KEYS, LANES), 0).astype(F32).astype(BF16)
    zero = jnp.zeros((PEER_KEYS, LANES), BF16)
    one = jnp.ones((PEER_KEYS, LANES), BF16)
    pack_rows = 2 * SUBLANES

    def row_plane(ref, t):
        tile = jnp.broadcast_to(ref[t:t + 1, :], (pack_rows, LANES)).astype(BF16)
        return jnp.concatenate([tile] * (PEER_KEYS // pack_rows), axis=0)

    def scatter_token(t):
        hit_a = row_plane(ia_s, t) == onehot_key
        hit_b = row_plane(ib_s, t) == onehot_key
        a_hi = jnp.where(hit_a, row_plane(ghi_s, t), zero)
        a_lo = jnp.where(hit_a, row_plane(glo_s, t), zero)
        b_one = jnp.where(hit_b, one, zero)
        lhs = jnp.concatenate([a_hi, a_lo], axis=1)
        rhs = jnp.concatenate([b_one, b_one], axis=1)
        w = lax.dot_general(lhs, rhs, _NT, preferred_element_type=F32)
        wbuf[pl.ds(t, PEER_KEYS, stride=W_PITCH), :] = w

    key = lax.broadcasted_iota(jnp.int32, (PEER_KEYS, tt), 0).astype(F32)
    per_head = tt // PEER_HEADS
    for hd in range(PEER_HEADS):
        one_head(hd, key)
        for t in range(hd * per_head, (hd + 1) * per_head):
            scatter_token(t)
    for ia in range(PEER_KEYS):
        w_ref[ia] = wbuf[ia * W_PITCH:ia * W_PITCH + tt, :].astype(BF16)

    ia_s[...] = iat_s[...].T
    ib_s[...] = ibt_s[...].T
    g = gt_s[...].T
    g_hi = g.astype(BF16).astype(F32)
    ghi_s[...] = g_hi
    glo_s[...] = g - g_hi


def _peer_route(h2, wq, sk):
    t, d = h2.shape
    tt = ROUTE_TILE
    n = t // tt
    kern = functools.partial(_peer_route_kernel, tt=tt)
    cspec = lambda a: pl.BlockSpec(a.shape, lambda i: (0,) * a.ndim)
    slab = pltpu.VMEM((PEER_HEADS * PEER_TOPK, tt), F32)
    tok = pltpu.VMEM((tt, PEER_HEADS * PEER_TOPK), F32)
    cur = lambda i: (jnp.minimum(i, n - 1), 0)
    return pl.pallas_call(
        kern,
        grid=(n + 1,),
        in_specs=[pl.BlockSpec((tt, d), cur),
                  cspec(wq), cspec(sk)],
        out_specs=pl.BlockSpec((PEER_KEYS, tt, PEER_KEYS),
                               lambda i: (0, jnp.maximum(i - 1, 0), 0)),
        out_shape=jax.ShapeDtypeStruct((PEER_KEYS, t, PEER_KEYS), BF16),
        scratch_shapes=[pltpu.VMEM((PEER_HEADS, 6, PEER_TOPK, tt), F32), slab, slab, slab, tok, tok, tok, tok,
                        pltpu.VMEM((PEER_KEYS * W_PITCH, PEER_KEYS), F32)],
        compiler_params=_cparams(1),
        name="peer_route",
    )(h2, wq, sk)


def _gelu(z):
    return 0.5 * z * (1.0 + lax.erf(z * (1.0 / math.sqrt(2.0))))


def _peer_dense_kernel(h_ref, w_ref, u_ref, v_ref, x_ref, mod_ref, *rest, n_ia, cast_next):
    if cast_next:
        uf_ref, vf_ref, o_ref, ub_ref, vb_ref, acc_ref = rest
        ub_ref[...] = uf_ref[...].astype(BF16)
        vb_ref[...] = vf_ref[...].astype(BF16)
    else:
        o_ref, acc_ref = rest
    j = pl.program_id(1)

    @pl.when(j == 0)
    def _():
        acc_ref[...] = jnp.zeros_like(acc_ref)

    z = lax.dot_general(h_ref[...], u_ref[...], _NT, preferred_element_type=F32)
    w = jnp.concatenate([w_ref[i] for i in range(n_ia)], axis=1)
    a = (_gelu(z) * w.astype(F32)).astype(BF16)
    acc_ref[...] += jnp.dot(a, v_ref[...], preferred_element_type=F32)

    @pl.when(j == pl.num_programs(1) - 1)
    def _():
        o_ref[...] = x_ref[...] + mod_ref[0][5:6] * acc_ref[...]


def _peer_dense(h_bf, w3, u_bf, v_bf, x, mod, seq, next_tables):
    t, d = h_bf.shape
    tt = min(DENSE_TILE, seq)
    n_ia = DENSE_SLABS
    e_blk = n_ia * PEER_KEYS
    ni, nj = t // tt, PEER_EXPERTS // e_blk
    cast_next = next_tables is not None
    kern = functools.partial(_peer_dense_kernel, n_ia=n_ia, cast_next=cast_next)
    in_specs = [
        pl.BlockSpec((tt, d), lambda i, j: (i, 0)),
        pl.BlockSpec((n_ia, tt, PEER_KEYS), lambda i, j: (j, i, 0)),
        pl.BlockSpec((e_blk, d), lambda i, j: (j, 0)),
        pl.BlockSpec((e_blk, d), lambda i, j: (j, 0)),
        pl.BlockSpec((tt, d), lambda i, j: (i, 0)),
        pl.BlockSpec((1, 6, d), lambda i, j: ((i * tt) // seq, 0, 0)),
    ]
    out_specs = [pl.BlockSpec((tt, d), lambda i, j: (i, 0))]
    out_shape = [jax.ShapeDtypeStruct((t, d), F32)]
    args = [h_bf, w3, u_bf, v_bf, x, mod]
    if cast_next:
        u_tab, v_tab, layer = next_tables
        step = lambda i, j: i * nj + j
        for tab in (u_tab, v_tab):
            t_in, t_out, t_shape = _table_cast_specs(tab, layer, ni * nj, step)
            in_specs.append(t_in)
            out_specs.append(t_out)
            out_shape.append(t_shape)
            args.append(tab)
    return pl.pallas_call(
        kern,
        grid=(ni, nj),
        in_specs=in_specs,
        out_specs=out_specs,
        out_shape=out_shape,
        scratch_shapes=[pltpu.VMEM((tt, d), F32)],
        compiler_params=_cparams(2),
        name="peer_dense",
    )(*args)


def _peer(h2, x, mod, w_q, subkeys, u_bf, v_bf, next_tables):
    b, s, d = x.shape
    h2f = h2.reshape(b * s, d)
    w3 = _peer_route(h2f, w_q.astype(BF16), subkeys.astype(BF16))
    out, *tables = _peer_dense(h2f, w3, u_bf, v_bf, x.reshape(b * s, d), mod, s, next_tables)
    return out.reshape(b, s, d), tables


def kernel(x, c, positions, tok_norm, ch_norm, ada_w, ada_b, conv_in_w, conv_w, conv_out_w,
           mla_in_w, q_a_norm, kv_a_norm, q_b_w, kv_b_w, q_norm, k_norm, mla_out_w,
           peer_q_w, peer_subkeys, peer_u, peer_v):
    b, s, d = x.shape
    depth = ada_w.shape[0]
    bp = -(-b // SUBLANES) * SUBLANES
    c_pad = jnp.zeros((bp, d), F32).at[:b].set(c)
    mod_all = _ada(c_pad, ada_w, ada_b)[:, :b].reshape(depth, b, 6, d)

    u_bf = v_bf = None
    row = lambda a: a.reshape(1, -1)
    for i in range(depth):
        mod = mod_all[i]
        j = i // 2
        if i % 2 == 0:
            x, h2, u_now, v_now = _conv_layer(x, mod, row(tok_norm[i]), row(ch_norm[i]),
                                              conv_in_w[j].astype(BF16), conv_w[j],
                                              conv_out_w[j].astype(BF16), peer_u, peer_v, i)
            if u_bf is None:
                u_bf, v_bf = u_now, v_now
        else:
            w_in = mla_in_w[j].astype(BF16)
            wq = w_in[:, :Q_LORA]
            wkv = w_in[:, Q_LORA:Q_LORA + KV_LORA]
            wkr = jnp.tile(w_in[:, Q_LORA + KV_LORA:], (1, LANES // QK_ROPE))
            wqb = q_b_w[j].astype(BF16).reshape(Q_LORA, MLA_HEADS, QK_HEAD)
            wqbn = wqb[:, :, :QK_NOPE].reshape(Q_LORA, MLA_HEADS * QK_NOPE)
            wqbr = wqb[:, :, QK_NOPE:].reshape(Q_LORA, MLA_HEADS * QK_ROPE)
            inv_freq = ROPE_THETA ** (-jnp.arange(0, QK_ROPE, 2, dtype=F32) / QK_ROPE)
            blocks = LANES // QK_ROPE
            half = QK_ROPE // 2
            freq = jnp.tile(inv_freq, 2 * blocks).reshape(1, LANES)
            sgn = jnp.tile(jnp.concatenate([-jnp.ones((half,), F32), jnp.ones((half,), F32)]),
                           blocks).reshape(1, LANES)
            q, k, v = _mla_proj(
                x, mod, row(tok_norm[i]), positions.reshape(b, s, 1), freq, sgn,
                wq, wkv, wkr, row(q_a_norm[j]), row(kv_a_norm[j]),
                wqbn, wqbr, kv_b_w[j].astype(BF16),
                row(q_norm[j][:QK_NOPE]), row(jnp.tile(q_norm[j][QK_NOPE:], blocks)),
                row(k_norm[j][:QK_NOPE]), row(jnp.tile(k_norm[j][QK_NOPE:], blocks)))
            o = _flash(q, k, v)
            x, h2 = _mla_out(o, x, mod, row(ch_norm[i]), mla_out_w[j].astype(BF16))
        next_tables = (peer_u, peer_v, i + 1) if i + 1 < depth else None
        x, tables = _peer(h2, x, mod, peer_q_w[i], peer_subkeys[i], u_bf, v_bf, next_tables)
        if tables:
            u_bf, v_bf = tables
    return x
```

```python
import functools
import math

import jax
import jax.numpy as jnp
from jax import lax
from jax.experimental import pallas as pl
from jax.experimental.pallas import tpu as pltpu

F32 = jnp.float32
BF16 = jnp.bfloat16

D_MODEL = 1024
MLA_HEADS = 8
Q_LORA = 384
KV_LORA = 256
QK_NOPE = 128
QK_ROPE = 64
QK_HEAD = QK_NOPE + QK_ROPE
QK_PAD = 256
V_HEAD = 128
ROPE_THETA = 10000.0
PEER_HEADS = 8
PEER_KEYS = 128
PEER_EXPERTS = PEER_KEYS * PEER_KEYS
PEER_HALF = 128
PEER_TOPK = 16
NORM_EPS = 1e-6

LANES = 128
SUBLANES = 8
VMEM_LIMIT = 56 * 1024 * 1024

SEQ_TILE = 512
ROUTE_TILE = 128
DENSE_TILE = 1024
DENSE_SLABS = SUBLANES

NEG_INF = float("-inf")
_NT = (((1,), (1,)), ((), ()))


def _cparams(n_axes):
    return pltpu.CompilerParams(
        dimension_semantics=("arbitrary",) * n_axes, vmem_limit_bytes=VMEM_LIMIT)


def _table_cast_specs(table, layer, n_steps, step_index):
    _, n, d = table.shape
    rows = n // n_steps
    assert rows * n_steps == n and rows % (2 * SUBLANES) == 0
    in_spec = pl.BlockSpec((None, rows, d), lambda *g: (layer, step_index(*g), 0))
    out_spec = pl.BlockSpec((rows, d), lambda *g: (step_index(*g), 0))
    return in_spec, out_spec, jax.ShapeDtypeStruct((n, d), BF16)


def _modulate(x, g, shift, scale):
    ms = jnp.mean(x * x, axis=-1, keepdims=True)
    y = x * lax.rsqrt(ms + NORM_EPS)
    return (y * g) * (1.0 + scale) + shift


def _ada_kernel(c_ref, w_ref, b_ref, o_ref):
    c = c_ref[...]
    ca = c * jax.nn.sigmoid(c)
    o_ref[0] = jnp.dot(ca, w_ref[0], preferred_element_type=F32,
                       precision=lax.Precision.HIGHEST) + b_ref[0]


def _ada(c_pad, ada_w, ada_b):
    depth, d, d6 = ada_w.shape
    bp = c_pad.shape[0]
    nb = d6 // d
    return pl.pallas_call(
        _ada_kernel,
        grid=(depth, nb),
        in_specs=[
            pl.BlockSpec((bp, d), lambda i, j: (0, 0)),
            pl.BlockSpec((1, d, d), lambda i, j: (i, 0, j)),
            pl.BlockSpec((1, 1, d), lambda i, j: (i, 0, j)),
        ],
        out_specs=pl.BlockSpec((1, bp, d), lambda i, j: (i, 0, j)),
        out_shape=jax.ShapeDtypeStruct((depth, bp, d6), F32),
        compiler_params=_cparams(2),
        name="ada",
    )(c_pad, ada_w, ada_b.reshape(depth, 1, d6))


def _conv_layer_kernel(x_ref, mod_ref, tn_ref, cn_ref, win_ref, cw_ref, wout_ref, uf_ref, vf_ref,
                       x1_ref, h2_ref, ub_ref, vb_ref, ubuf, *, ts):
    ub_ref[...] = uf_ref[...].astype(BF16)
    vb_ref[...] = vf_ref[...].astype(BF16)

    d = D_MODEL
    x = x_ref[0]
    m = mod_ref[0]
    sh1, sc1, g1, sh2, sc2 = m[0:1], m[1:2], m[2:3], m[3:4], m[4:5]
    h = _modulate(x, tn_ref[...], sh1, sc1)
    bcv = jnp.dot(h.astype(BF16), win_ref[...], preferred_element_type=F32)
    b_gate = bcv[:, :d]
    u = bcv[:, d:2 * d] * bcv[:, 2 * d:]

    @pl.when(pl.program_id(1) == 0)
    def _():
        ubuf[0:SUBLANES, :] = jnp.zeros((SUBLANES, d), F32)

    ubuf[SUBLANES:SUBLANES + ts, :] = u
    u1 = ubuf[SUBLANES - 1:SUBLANES - 1 + ts, :]
    u2 = ubuf[SUBLANES - 2:SUBLANES - 2 + ts, :]
    cw = cw_ref[...]
    conv = cw[0:1] * u2 + cw[1:2] * u1 + cw[2:3] * u
    ubuf[0:SUBLANES, :] = u[ts - SUBLANES:, :]
    y = jnp.dot((b_gate * conv).astype(BF16), wout_ref[...], preferred_element_type=F32)
    x1 = x + g1 * y
    x1_ref[0] = x1
    h2_ref[0] = _modulate(x1, cn_ref[...], sh2, sc2).astype(BF16)


def _conv_layer(x, mod, tok_norm, ch_norm, w_in, conv_w, w_out, u_tab, v_tab, layer):
    b, s, d = x.shape
    ts = min(SEQ_TILE, s)
    nj = s // ts
    kern = functools.partial(_conv_layer_kernel, ts=ts)
    const = lambda *shape: pl.BlockSpec(shape, lambda i, j: (0,) * len(shape))
    step = lambda i, j: i * nj + j
    u_in, u_out, u_shape = _table_cast_specs(u_tab, layer, b * nj, step)
    v_in, v_out, v_shape = _table_cast_specs(v_tab, layer, b * nj, step)
    return pl.pallas_call(
        kern,
        grid=(b, nj),
        in_specs=[
            pl.BlockSpec((1, ts, d), lambda i, j: (i, j, 0)),
            pl.BlockSpec((1, 6, d), lambda i, j: (i, 0, 0)),
            const(1, d), const(1, d), const(d, 3 * d), const(3, d), const(d, d), u_in, v_in,
        ],
        out_specs=[pl.BlockSpec((1, ts, d), lambda i, j: (i, j, 0))] * 2 + [u_out, v_out],
        out_shape=[jax.ShapeDtypeStruct((b, s, d), F32), jax.ShapeDtypeStruct((b, s, d), BF16),
                   u_shape, v_shape],
        scratch_shapes=[pltpu.VMEM((ts + SUBLANES, d), F32)],
        compiler_params=_cparams(2),
        name="conv_layer",
    )(x, mod, tok_norm, ch_norm, w_in, conv_w, w_out, u_tab, v_tab)


def _mla_proj_kernel(x_ref, mod_ref, tn_ref, pos_ref, freq_ref, sgn_ref,
                     wq_ref, wkv_ref, wkr_ref, qag_ref, kvag_ref,
                     wqbn_ref, wqbr_ref, wkvb_ref, qgn_ref, qgr_ref, kgn_ref, kgr_ref,
                     q_ref, k_ref, v_ref, *, ts):
    x = x_ref[0]
    m = mod_ref[0]
    h = _modulate(x, tn_ref[...], m[0:1], m[1:2]).astype(BF16)
    q_lat = jnp.dot(h, wq_ref[...], preferred_element_type=F32)
    kv_lat = jnp.dot(h, wkv_ref[...], preferred_element_type=F32)
    k_rope = jnp.dot(h, wkr_ref[...], preferred_element_type=F32)

    def rms(v, g):
        return v * lax.rsqrt(jnp.mean(v * v, axis=-1, keepdims=True) + NORM_EPS) * g

    qn = rms(q_lat, qag_ref[...]).astype(BF16)
    kvn = rms(kv_lat, kvag_ref[...]).astype(BF16)
    q_nope = jnp.dot(qn, wqbn_ref[...], preferred_element_type=F32)
    q_rope = jnp.dot(qn, wqbr_ref[...], preferred_element_type=F32)
    kv = jnp.dot(kvn, wkvb_ref[...], preferred_element_type=F32)

    ang = pos_ref[0].astype(F32) * freq_ref[...]
    cos = jnp.cos(ang)
    sin_signed = jnp.sin(ang) * sgn_ref[...]
    lane = lax.broadcasted_iota(jnp.int32, (ts, LANES), 1)
    first_half = (lane % QK_ROPE) < QK_ROPE // 2
    block_of = [lane < QK_ROPE, lane >= QK_ROPE]

    def rope(v):
        rot = jnp.where(first_half, pltpu.roll(v, LANES - QK_ROPE // 2, axis=1),
                        pltpu.roll(v, QK_ROPE // 2, axis=1))
        return v * cos + rot * sin_signed

    scale = QK_HEAD ** -0.5 * math.log2(math.e)
    kr_g = rope(k_rope * kgr_ref[...])
    kr_ss = 0.5 * jnp.sum(k_rope * k_rope, axis=-1, keepdims=True)
    pair_raw, pair_g = None, None
    for hd in range(MLA_HEADS):
        own = block_of[hd % 2]
        if hd % 2 == 0:
            pair_raw = q_rope[:, (hd // 2) * LANES:(hd // 2 + 1) * LANES]
            pair_g = rope(pair_raw * qgr_ref[...])
        qn_h = q_nope[:, hd * QK_NOPE:(hd + 1) * QK_NOPE]
        qr_h = jnp.where(own, pair_raw, 0.0)
        ss = jnp.sum(qn_h * qn_h + qr_h * qr_h, axis=-1, keepdims=True)
        rinv = lax.rsqrt(ss * (1.0 / QK_HEAD) + NORM_EPS) * scale
        qa = qn_h * rinv * qgn_ref[...]
        qb = jnp.where(own, pair_g, 0.0) * rinv
        q_ref[0, hd] = jnp.concatenate([qa, qb], axis=1).astype(BF16)

        kn_h = kv[:, hd * 2 * QK_NOPE:hd * 2 * QK_NOPE + QK_NOPE]
        v_h = kv[:, hd * 2 * QK_NOPE + QK_NOPE:(hd + 1) * 2 * QK_NOPE]
        kss = jnp.sum(kn_h * kn_h, axis=-1, keepdims=True) + kr_ss
        krinv = lax.rsqrt(kss * (1.0 / QK_HEAD) + NORM_EPS)
        ka = kn_h * krinv * kgn_ref[...]
        kb = jnp.where(own, kr_g, 0.0) * krinv
        k_ref[0, hd] = jnp.concatenate([ka, kb], axis=1).astype(BF16)
        v_ref[0, hd] = v_h.astype(BF16)


def _mla_proj(x, mod, tok_norm, pos3, freq, sgn, wq, wkv, wkr, qag, kvag,
              wqbn, wqbr, wkvb, qgn, qgr, kgn, kgr):
    b, s, d = x.shape
    ts = min(SEQ_TILE, s)
    kern = functools.partial(_mla_proj_kernel, ts=ts)
    consts = [tok_norm]
    rest = [freq, sgn, wq, wkv, wkr, qag, kvag, wqbn, wqbr, wkvb, qgn, qgr, kgn, kgr]
    cspec = lambda a: pl.BlockSpec(a.shape, lambda i, j: (0,) * a.ndim)
    hq = pl.BlockSpec((1, MLA_HEADS, ts, QK_PAD), lambda i, j: (i, 0, j, 0))
    hv = pl.BlockSpec((1, MLA_HEADS, ts, V_HEAD), lambda i, j: (i, 0, j, 0))
    return pl.pallas_call(
        kern,
        grid=(b, s // ts),
        in_specs=[
            pl.BlockSpec((1, ts, d), lambda i, j: (i, j, 0)),
            pl.BlockSpec((1, 6, d), lambda i, j: (i, 0, 0)),
            cspec(tok_norm),
            pl.BlockSpec((1, ts, 1), lambda i, j: (i, j, 0)),
        ] + [cspec(a) for a in rest],
        out_specs=[hq, hq, hv],
        out_shape=[
            jax.ShapeDtypeStruct((b, MLA_HEADS, s, QK_PAD), BF16),
            jax.ShapeDtypeStruct((b, MLA_HEADS, s, QK_PAD), BF16),
            jax.ShapeDtypeStruct((b, MLA_HEADS, s, V_HEAD), BF16),
        ],
        compiler_params=_cparams(2),
        name="mla_proj",
    )(x, mod, tok_norm, pos3, *rest)


FLASH_HEADS = 4


def _flash_kernel(q_ref, k_ref, v_ref, o_ref, *, tq):
    i = pl.program_id(2)

    def block(j, carries, diagonal):
        start = pl.multiple_of(j * tq, tq)
        out = []
        for hh in range(FLASH_HEADS):
            m, l, acc = carries[hh]
            kj = k_ref[0, hh, pl.ds(start, tq), :]
            vj = v_ref[0, hh, pl.ds(start, tq), :]
            s = lax.dot_general(q_ref[0, hh], kj, _NT, preferred_element_type=F32)
            if diagonal:
                row = lax.broadcasted_iota(jnp.int32, (tq, tq), 0)
                col = lax.broadcasted_iota(jnp.int32, (tq, tq), 1)
                s = jnp.where(row >= col, s, NEG_INF)
            m_new = jnp.maximum(m, jnp.max(s, axis=-1, keepdims=True))
            p = jnp.exp2(s - m_new)
            alpha = jnp.exp2(m - m_new)
            l = alpha * l + jnp.sum(p, axis=-1, keepdims=True)
            acc = alpha * acc + jnp.dot(p.astype(BF16), vj, preferred_element_type=F32)
            out.append((m_new, l, acc))
        return tuple(out)

    init = tuple((jnp.full((tq, 1), NEG_INF, F32), jnp.zeros((tq, 1), F32),
                  jnp.zeros((tq, V_HEAD), F32)) for _ in range(FLASH_HEADS))
    carries = lax.fori_loop(0, i, lambda j, c: block(j, c, False), init)
    carries = block(i, carries, True)
    o_ref[0] = jnp.concatenate([(acc / l).astype(BF16) for _, l, acc in carries], axis=1)


def _flash(q, k, v):
    b, hds, s, _ = q.shape
    tq = min(SEQ_TILE, s)
    kern = functools.partial(_flash_kernel, tq=tq)
    fh = FLASH_HEADS
    return pl.pallas_call(
        kern,
        grid=(b, hds // fh, s // tq),
        in_specs=[
            pl.BlockSpec((1, fh, tq, QK_PAD), lambda bi, h, i: (bi, h, i, 0)),
            pl.BlockSpec((1, fh, s, QK_PAD), lambda bi, h, i: (bi, h, 0, 0)),
            pl.BlockSpec((1, fh, s, V_HEAD), lambda bi, h, i: (bi, h, 0, 0)),
        ],
        out_specs=pl.BlockSpec((1, tq, fh * V_HEAD), lambda bi, h, i: (bi, i, h)),
        out_shape=jax.ShapeDtypeStruct((b, s, hds * V_HEAD), BF16),
        compiler_params=_cparams(3),
        name="mla_flash",
    )(q, k, v)


def _mla_out_kernel(o_ref, x_ref, mod_ref, cn_ref, w_ref, x1_ref, h2_ref):
    m = mod_ref[0]
    y = jnp.dot(o_ref[0], w_ref[...], preferred_element_type=F32)
    x1 = x_ref[0] + m[2:3] * y
    x1_ref[0] = x1
    h2_ref[0] = _modulate(x1, cn_ref[...], m[3:4], m[4:5]).astype(BF16)


def _mla_out(o, x, mod, ch_norm, w_out):
    b, s, d = x.shape
    ts = min(SEQ_TILE, s)
    return pl.pallas_call(
        _mla_out_kernel,
        grid=(b, s // ts),
        in_specs=[
            pl.BlockSpec((1, ts, d), lambda i, j: (i, j, 0)),
            pl.BlockSpec((1, ts, d), lambda i, j: (i, j, 0)),
            pl.BlockSpec((1, 6, d), lambda i, j: (i, 0, 0)),
            pl.BlockSpec((1, d), lambda i, j: (0, 0)),
            pl.BlockSpec((d, d), lambda i, j: (0, 0)),
        ],
        out_specs=[pl.BlockSpec((1, ts, d), lambda i, j: (i, j, 0))] * 2,
        out_shape=[jax.ShapeDtypeStruct((b, s, d), F32), jax.ShapeDtypeStruct((b, s, d), BF16)],
        compiler_params=_cparams(2),
        name="mla_out",
    )(o, x, mod, ch_norm, w_out)


def _sort16_network():
    def merge(lo, hi, r):
        step = r * 2
        if step < hi - lo:
            yield from merge(lo, hi, step)
            yield from merge(lo + r, hi, step)
            yield from ((i, i + r) for i in range(lo + r, hi - r, step))
        else:
            yield (lo, lo + r)

    def sort(lo, hi):
        if hi - lo >= 1:
            mid = lo + (hi - lo) // 2
            yield from sort(lo, mid)
            yield from sort(mid + 1, hi)
            yield from merge(lo, hi, 1)

    return tuple(sort(0, PEER_TOPK - 1))


_SORT16 = _sort16_network()


def _top16_keys(s, key, stack):
    n, l = s.shape
    depth = n // SUBLANES
    val = [s[SUBLANES * v:SUBLANES * (v + 1), :] for v in range(depth)]
    pay = [key[SUBLANES * v:SUBLANES * (v + 1), :] for v in range(depth)]
    for a, b in _SORT16:
        swap = (val[b] > val[a]) | ((val[b] == val[a]) & (pay[b] < pay[a]))
        val[a], val[b] = jnp.where(swap, val[b], val[a]), jnp.where(swap, val[a], val[b])
        pay[a], pay[b] = jnp.where(swap, pay[b], pay[a]), jnp.where(swap, pay[a], pay[b])
    vals, idxs = [], []
    for r in range(PEER_TOPK):
        mx = jnp.max(val[0], axis=0, keepdims=True)
        idx = jnp.min(jnp.where(val[0] == mx, pay[0], float(n)), axis=0, keepdims=True)
        hit = pay[0] == idx
        vals.append(mx)
        idxs.append(idx)
        for p in range(PEER_TOPK - 1 - r):
            val[p] = jnp.where(hit, val[p + 1], val[p])
            pay[p] = jnp.where(hit, pay[p + 1], pay[p])
    return stack(vals), stack(idxs)


def _top16_pairs(s0, i0, s1, i1, stack):
    l = s0.shape[1]
    sub = lax.broadcasted_iota(jnp.int32, (SUBLANES, l), 0)
    subf = sub.astype(F32)
    lists = []
    for p in range(PEER_TOPK):
        plane = s0[0:SUBLANES] + s1[p:p + 1]
        n_valid = PEER_TOPK // (p + 1)
        if n_valid < SUBLANES:
            plane = jnp.where(sub < n_valid, plane, NEG_INF)
        lists.append(plane)
    tail = s0[SUBLANES:] + s1[0:1]
    pos_tail = (subf + float(SUBLANES)) * float(PEER_TOPK)
    pops = jnp.zeros((SUBLANES, l), F32)
    no_pos = float(PEER_TOPK * PEER_TOPK)
    vals, poss = [], []
    for r in range(PEER_TOPK):
        pos_head = subf * float(PEER_TOPK) + pops
        mx = jnp.max(jnp.maximum(lists[0], tail), axis=0, keepdims=True)
        cand = jnp.minimum(jnp.where(lists[0] == mx, pos_head, no_pos),
                           jnp.where(tail == mx, pos_tail, no_pos))
        pos = jnp.min(cand, axis=0, keepdims=True)
        hit = pos_head == pos
        vals.append(mx)
        poss.append(pos)
        pops = pops + jnp.where(hit, 1.0, 0.0)
        for p in range(PEER_TOPK - 1 - r):
            lists[p] = jnp.where(hit, lists[p + 1], lists[p])
        tail = jnp.where(pos_tail == pos, NEG_INF, tail)
    vals, poss = stack(vals), stack(poss)
    pi = jnp.floor(poss * (1.0 / PEER_TOPK))
    pj = poss - pi * float(PEER_TOPK)
    ia = jnp.zeros((PEER_TOPK, l), F32)
    ib = jnp.zeros((PEER_TOPK, l), F32)
    for k in range(PEER_TOPK):
        ia = jnp.where(pi == float(k), i0[k:k + 1], ia)
        ib = jnp.where(pj == float(k), i1[k:k + 1], ib)
    return vals, ia, ib


W_PITCH = 136


def _peer_route_kernel(h_ref, wq_ref, sk_ref, w_ref,
                       rows_s, iat_s, ibt_s, gt_s, ia_s, ib_s, ghi_s, glo_s, wbuf, *, tt):

    @pl.when(pl.program_id(0) == 0)
    def _():
        for ref in (ia_s, ib_s, ghi_s, glo_s):
            ref[...] = jnp.zeros_like(ref)

    def one_head(hd, key):
        planes = iter(range(rows_s.shape[1]))

        def stack(rows):
            plane = rows_s.at[hd, next(planes)]
            for r, row in enumerate(rows):
                plane[r:r + 1, :] = row
            return plane[...]

        cols = slice(hd * 2 * PEER_HALF, (hd + 1) * 2 * PEER_HALF)
        q = jnp.dot(h_ref[...], wq_ref[:, cols], preferred_element_type=F32).astype(BF16)
        tops = []
        for p in range(2):
            s = lax.dot_general(sk_ref[hd, p], q[:, p * PEER_HALF:(p + 1) * PEER_HALF], _NT,
                                preferred_element_type=F32)
            tops.append(_top16_keys(s, key, stack))
        (s0, i0), (s1, i1) = tops
        best, ia, ib = _top16_pairs(s0, i0, s1, i1, stack)
        e = jnp.exp(best - best[0:1])
        g = e / jnp.sum(e, axis=0, keepdims=True)
        rows = pl.ds(pl.multiple_of(hd * PEER_TOPK, PEER_TOPK), PEER_TOPK)
        iat_s[rows, :] = ia
        ibt_s[rows, :] = ib
        gt_s[rows, :] = g

    onehot_key = lax.broadcasted_iota(jnp.int32, (PEER_KEYS, LANES), 0).astype(F32).astype(BF16)
    zero = jnp.zeros((PEER_KEYS, LANES), BF16)
    one = jnp.ones((PEER_KEYS, LANES), BF16)
    pack_rows = 2 * SUBLANES

    def row_plane(ref, t):
        tile = jnp.broadcast_to(ref[t:t + 1, :], (pack_rows, LANES)).astype(BF16)
        return jnp.concatenate([tile] * (PEER_KEYS // pack_rows), axis=0)

    def scatter_token(t):
        hit_a = row_plane(ia_s, t) == onehot_key
        hit_b = row_plane(ib_s, t) == onehot_key
        a_hi = jnp.where(hit_a, row_plane(ghi_s, t), zero)
        a_lo = jnp.where(hit_a, row_plane(glo_s, t), zero)
        b_one = jnp.where(hit_b, one, zero)
        lhs = jnp.concatenate([a_hi, a_lo], axis=1)
        rhs = jnp.concatenate([b_one, b_one], axis=1)
        w = lax.dot_general(lhs, rhs, _NT, preferred_element_type=F32)
        wbuf[pl.ds(t, PEER_KEYS, stride=W_PITCH), :] = w

    key = lax.broadcasted_iota(jnp.int32, (PEER_KEYS, tt), 0).astype(F32)
    per_head = tt // PEER_HEADS
    for hd in range(PEER_HEADS):
        one_head(hd, key)
        for t in range(hd * per_head, (hd + 1) * per_head):
            scatter_token(t)
    for ia in range(PEER_KEYS):
        w_ref[ia] = wbuf[ia * W_PITCH:ia * W_PITCH + tt, :].astype(BF16)

    ia_s[...] = iat_s[...].T
    ib_s[...] = ibt_s[...].T
    g = gt_s[...].T
    g_hi = g.astype(BF16).astype(F32)
    ghi_s[...] = g_hi
    glo_s[...] = g - g_hi


def _peer_route(h2, wq, sk):
    t, d = h2.shape
    tt = ROUTE_TILE
    n = t // tt
    kern = functools.partial(_peer_route_kernel, tt=tt)
    cspec = lambda a: pl.BlockSpec(a.shape, lambda i: (0,) * a.ndim)
    slab = pltpu.VMEM((PEER_HEADS * PEER_TOPK, tt), F32)
    tok = pltpu.VMEM((tt, PEER_HEADS * PEER_TOPK), F32)
    cur = lambda i: (jnp.minimum(i, n - 1), 0)
    return pl.pallas_call(
        kern,
        grid=(n + 1,),
        in_specs=[pl.BlockSpec((tt, d), cur),
                  cspec(wq), cspec(sk)],
        out_specs=pl.BlockSpec((PEER_KEYS, tt, PEER_KEYS),
                               lambda i: (0, jnp.maximum(i - 1, 0), 0)),
        out_shape=jax.ShapeDtypeStruct((PEER_KEYS, t, PEER_KEYS), BF16),
        scratch_shapes=[pltpu.VMEM((PEER_HEADS, 6, PEER_TOPK, tt), F32), slab, slab, slab, tok, tok, tok, tok,
                        pltpu.VMEM((PEER_KEYS * W_PITCH, PEER_KEYS), F32)],
        compiler_params=_cparams(1),
        name="peer_route",
    )(h2, wq, sk)


def _gelu(z):
    return 0.5 * z * (1.0 + lax.erf(z * (1.0 / math.sqrt(2.0))))


def _peer_dense_kernel(h_ref, w_ref, u_ref, v_ref, x_ref, mod_ref, *rest, n_ia, cast_next):
    if cast_next:
        uf_ref, vf_ref, o_ref, ub_ref, vb_ref, acc_ref = rest
        ub_ref[...] = uf_ref[...].astype(BF16)
        vb_ref[...] = vf_ref[...].astype(BF16)
    else:
        o_ref, acc_ref = rest
    j = pl.program_id(1)

    @pl.when(j == 0)
    def _():
        acc_ref[...] = jnp.zeros_like(acc_ref)

    half_ia = n_ia // 2
    half = half_ia * PEER_KEYS
    chunks = []
    for c in range(2):
        z = lax.dot_general(h_ref[...], u_ref[c * half:(c + 1) * half, :], _NT,
                            preferred_element_type=F32)
        w = jnp.concatenate([w_ref[c * half_ia + i] for i in range(half_ia)], axis=1)
        chunks.append((_gelu(z) * w.astype(F32)).astype(BF16))
    a = jnp.concatenate(chunks, axis=1)
    acc_ref[...] += jnp.dot(a, v_ref[...], preferred_element_type=F32)

    @pl.when(j == pl.num_programs(1) - 1)
    def _():
        o_ref[...] = x_ref[...] + mod_ref[0][5:6] * acc_ref[...]


def _peer_dense(h_bf, w3, u_bf, v_bf, x, mod, seq, next_tables):
    t, d = h_bf.shape
    tt = min(DENSE_TILE, seq)
    n_ia = DENSE_SLABS
    e_blk = n_ia * PEER_KEYS
    ni, nj = t // tt, PEER_EXPERTS // e_blk
    cast_next = next_tables is not None
    kern = functools.partial(_peer_dense_kernel, n_ia=n_ia, cast_next=cast_next)
    in_specs = [
        pl.BlockSpec((tt, d), lambda i, j: (i, 0)),
        pl.BlockSpec((n_ia, tt, PEER_KEYS), lambda i, j: (j, i, 0)),
        pl.BlockSpec((e_blk, d), lambda i, j: (j, 0)),
        pl.BlockSpec((e_blk, d), lambda i, j: (j, 0)),
        pl.BlockSpec((tt, d), lambda i, j: (i, 0)),
        pl.BlockSpec((1, 6, d), lambda i, j: ((i * tt) // seq, 0, 0)),
    ]
    out_specs = [pl.BlockSpec((tt, d), lambda i, j: (i, 0))]
    out_shape = [jax.ShapeDtypeStruct((t, d), F32)]
    args = [h_bf, w3, u_bf, v_bf, x, mod]
    if cast_next:
        u_tab, v_tab, layer = next_tables
        step = lambda i, j: i * nj + j
        for tab in (u_tab, v_tab):
            t_in, t_out, t_shape = _table_cast_specs(tab, layer, ni * nj, step)
            in_specs.append(t_in)
            out_specs.append(t_out)
            out_shape.append(t_shape)
            args.append(tab)
    return pl.pallas_call(
        kern,
        grid=(ni, nj),
        in_specs=in_specs,
        out_specs=out_specs,
        out_shape=out_shape,
        scratch_shapes=[pltpu.VMEM((tt, d), F32)],
        compiler_params=_cparams(2),
        name="peer_dense",
    )(*args)


def _peer(h2, x, mod, w_q, subkeys, u_bf, v_bf, next_tables):
    b, s, d = x.shape
    h2f = h2.reshape(b * s, d)
    w3 = _peer_route(h2f, w_q.astype(BF16), subkeys.astype(BF16))
    out, *tables = _peer_dense(h2f, w3, u_bf, v_bf, x.reshape(b * s, d), mod, s, next_tables)
    return out.reshape(b, s, d), tables


def kernel(x, c, positions, tok_norm, ch_norm, ada_w, ada_b, conv_in_w, conv_w, conv_out_w,
           mla_in_w, q_a_norm, kv_a_norm, q_b_w, kv_b_w, q_norm, k_norm, mla_out_w,
           peer_q_w, peer_subkeys, peer_u, peer_v):
    b, s, d = x.shape
    depth = ada_w.shape[0]
    bp = -(-b // SUBLANES) * SUBLANES
    c_pad = jnp.zeros((bp, d), F32).at[:b].set(c)
    mod_all = _ada(c_pad, ada_w, ada_b)[:, :b].reshape(depth, b, 6, d)

    u_bf = v_bf = None
    row = lambda a: a.reshape(1, -1)
    for i in range(depth):
        mod = mod_all[i]
        j = i // 2
        if i % 2 == 0:
            x, h2, u_now, v_now = _conv_layer(x, mod, row(tok_norm[i]), row(ch_norm[i]),
                                              conv_in_w[j].astype(BF16), conv_w[j],
                                              conv_out_w[j].astype(BF16), peer_u, peer_v, i)
            if u_bf is None:
                u_bf, v_bf = u_now, v_now
        else:
            w_in = mla_in_w[j].astype(BF16)
            wq = w_in[:, :Q_LORA]
            wkv = w_in[:, Q_LORA:Q_LORA + KV_LORA]
            wkr = jnp.tile(w_in[:, Q_LORA + KV_LORA:], (1, LANES // QK_ROPE))
            wqb = q_b_w[j].astype(BF16).reshape(Q_LORA, MLA_HEADS, QK_HEAD)
            wqbn = wqb[:, :, :QK_NOPE].reshape(Q_LORA, MLA_HEADS * QK_NOPE)
            wqbr = wqb[:, :, QK_NOPE:].reshape(Q_LORA, MLA_HEADS * QK_ROPE)
            inv_freq = ROPE_THETA ** (-jnp.arange(0, QK_ROPE, 2, dtype=F32) / QK_ROPE)
            blocks = LANES // QK_ROPE
            half = QK_ROPE // 2
            freq = jnp.tile(inv_freq, 2 * blocks).reshape(1, LANES)
            sgn = jnp.tile(jnp.concatenate([-jnp.ones((half,), F32), jnp.ones((half,), F32)]),
                           blocks).reshape(1, LANES)
            q, k, v = _mla_proj(
                x, mod, row(tok_norm[i]), positions.reshape(b, s, 1), freq, sgn,
                wq, wkv, wkr, row(q_a_norm[j]), row(kv_a_norm[j]),
                wqbn, wqbr, kv_b_w[j].astype(BF16),
                row(q_norm[j][:QK_NOPE]), row(jnp.tile(q_norm[j][QK_NOPE:], blocks)),
                row(k_norm[j][:QK_NOPE]), row(jnp.tile(k_norm[j][QK_NOPE:], blocks)))
            o = _flash(q, k, v)
            x, h2 = _mla_out(o, x, mod, row(ch_norm[i]), mla_out_w[j].astype(BF16))
        next_tables = (peer_u, peer_v, i + 1) if i + 1 < depth else None
        x, tables = _peer(h2, x, mod, peer_q_w[i], peer_subkeys[i], u_bf, v_bf, next_tables)
        if tables:
            u_bf, v_bf = tables
    return x
```
